```python
import jax, jax.numpy as jnp
from jax import lax
import numpy as np

D_MODEL = 1024
BATCH = 1
SEQ = 16384
DEPTH = 2

HEAD_DIM = 64
MOBA_HEADS = 8
MOBA_BLOCK = 256
MOBA_TOPK = 3
SB_HEADS = 8
MLA_HEADS = 8
MLA_Q_LORA = 256
MLA_KV_LORA = 128
MLA_NOPE = 64
MLA_ROPE = 32
MLA_V = 64
MLA_QK = MLA_NOPE + MLA_ROPE
ROPE_THETA = 500000.0
PARTIAL_ROT = HEAD_DIM // 4
Q_BLOCK = 128
D_FF = 2816
N_BRANCH = 3
BRANCH_W = 512
N_SUB = 3
EPS = 1e-6

MOBA_W = MOBA_HEADS * HEAD_DIM
SB_W = SB_HEADS * HEAD_DIM
SPLIT_1 = 3 * MOBA_W
SPLIT_2 = SPLIT_1 + 3 * SB_W
SPLIT_3 = SPLIT_2 + MLA_Q_LORA
SPLIT_4 = SPLIT_3 + MLA_KV_LORA
SPLIT_5 = SPLIT_4 + MLA_ROPE
N_IN = SPLIT_5 + N_BRANCH * D_MODEL

kernel_name = 'hybrid_moba_stickbreak_mla_macaron'


def rms_norm(x, g):
    xf = x.astype(jnp.float32)
    y = xf * lax.rsqrt(jnp.mean(xf * xf, axis=-1, keepdims=True) + EPS)
    return (y * g.astype(jnp.float32)).astype(x.dtype)


def rope_tables(n_rot, seq):
    inv = jnp.power(jnp.float32(ROPE_THETA), -jnp.arange(0, n_rot, 2, dtype=jnp.float32) / n_rot)
    ang = jnp.arange(seq, dtype=jnp.float32)[:, None] * inv[None, :]
    return jnp.cos(ang), jnp.sin(ang)


def apply_rope(x, cos, sin):
    x1, x2 = jnp.split(x, 2, axis=-1)
    cos = cos.astype(x.dtype)
    sin = sin.astype(x.dtype)
    return jnp.concatenate([x1 * cos - x2 * sin, x1 * sin + x2 * cos], axis=-1)


def partial_rope(x, cos, sin):
    return jnp.concatenate([apply_rope(x[..., :PARTIAL_ROT], cos, sin), x[..., PARTIAL_ROT:]], axis=-1)


def to_heads(x, n_heads):
    b, s, _ = x.shape
    return x.reshape(b, s, n_heads, -1).transpose(0, 2, 1, 3)


def from_heads(x):
    b, h, s, d = x.shape
    return x.transpose(0, 2, 1, 3).reshape(b, s, h * d)


def query_blocks(q):
    b, h, s, d = q.shape
    nq = s // Q_BLOCK
    return q.reshape(b, h, nq, Q_BLOCK, d).transpose(2, 0, 1, 3, 4), nq


def merge_query_blocks(o):
    nq, b, h, qb, d = o.shape
    return o.transpose(1, 2, 0, 3, 4).reshape(b, h, nq * qb, d)


def causal_softmax_attention(q, k, v):
    s_len = k.shape[2]
    scale = q.shape[-1] ** -0.5
    qb, nq = query_blocks(q)
    kpos = jnp.arange(s_len)

    def one(args):
        qc, i = args
        sc = jnp.einsum('bhqd,bhkd->bhqk', qc, k).astype(jnp.float32) * scale
        qpos = i * Q_BLOCK + jnp.arange(Q_BLOCK)
        sc = jnp.where(kpos[None, :] <= qpos[:, None], sc, -jnp.inf)
        p = jax.nn.softmax(sc, axis=-1).astype(v.dtype)
        return jnp.einsum('bhqk,bhkd->bhqd', p, v)

    return merge_query_blocks(lax.map(one, (qb, jnp.arange(nq))))


def stick_breaking_attention(q, k, v):
    s_len = k.shape[2]
    scale = q.shape[-1] ** -0.5
    qb, nq = query_blocks(q)
    kpos = jnp.arange(s_len)

    def one(args):
        qc, i = args
        z = jnp.einsum('bhqd,bhkd->bhqk', qc, k).astype(jnp.float32) * scale
        qpos = i * Q_BLOCK + jnp.arange(Q_BLOCK)
        mask = kpos[None, :] < qpos[:, None]
        log_1m = jnp.where(mask, jax.nn.log_sigmoid(-z), 0.0)
        suffix = lax.cumsum(log_1m, axis=3, reverse=True) - log_1m
        w = jnp.where(mask, jnp.exp(jax.nn.log_sigmoid(z) + suffix), 0.0).astype(v.dtype)
        return jnp.einsum('bhqk,bhkd->bhqd', w, v)

    return merge_query_blocks(lax.map(one, (qb, jnp.arange(nq))))


def moba_attention(q, k, v):
    b, h, s_len, d = q.shape
    n = b * h
    scale = d ** -0.5
    s_pad = -(-s_len // MOBA_BLOCK) * MOBA_BLOCK
    nb = s_pad // MOBA_BLOCK
    top = min(MOBA_TOPK, nb)
    qf = q.reshape(n, s_len, d)
    pad = ((0, 0), (0, s_pad - s_len), (0, 0))
    kb = jnp.pad(k.reshape(n, s_len, d), pad).reshape(n, nb, MOBA_BLOCK, d)
    vb = jnp.pad(v.reshape(n, s_len, d), pad).reshape(n, nb, MOBA_BLOCK, d)
    k_mean = jnp.mean(kb.astype(jnp.float32), axis=2)
    gate = jnp.einsum('nsd,nbd->nsb', qf.astype(jnp.float32), k_mean)
    qblk = jnp.arange(s_len) // MOBA_BLOCK
    past = jnp.arange(nb)[None, :] < qblk[:, None]
    gate = jnp.where(past[None], gate, -jnp.inf)
    _, sel = lax.top_k(gate, top)
    valid = sel < qblk[None, :, None]
    nq = s_len // Q_BLOCK
    q_all = qf.reshape(n, nq, Q_BLOCK, d).transpose(1, 0, 2, 3)
    sel_all = sel.reshape(n, nq, Q_BLOCK, top).transpose(1, 0, 2, 3)
    val_all = valid.reshape(n, nq, Q_BLOCK, top).transpose(1, 0, 2, 3)
    rows = jnp.arange(n)[:, None, None]

    def one(args):
        qc, sc, vc, i = args
        ks = kb[rows, sc]
        vs = vb[rows, sc]
        own = (i * Q_BLOCK) // MOBA_BLOCK
        ko = lax.dynamic_index_in_dim(kb, own, axis=1, keepdims=False)
        vo = lax.dynamic_index_in_dim(vb, own, axis=1, keepdims=False)
        s_sel = jnp.einsum('nqd,nqkjd->nqkj', qc, ks).astype(jnp.float32) * scale
        s_sel = jnp.where(vc[..., None], s_sel, -jnp.inf).reshape(n, Q_BLOCK, top * MOBA_BLOCK)
        s_own = jnp.einsum('nqd,njd->nqj', qc, ko).astype(jnp.float32) * scale
        qpos = i * Q_BLOCK + jnp.arange(Q_BLOCK)
        kpos = own * MOBA_BLOCK + jnp.arange(MOBA_BLOCK)
        s_own = jnp.where(kpos[None, :] <= qpos[:, None], s_own, -jnp.inf)
        p = jax.nn.softmax(jnp.concatenate([s_sel, s_own], axis=-1), axis=-1).astype(v.dtype)
        p_sel = p[..., :top * MOBA_BLOCK].reshape(n, Q_BLOCK, top, MOBA_BLOCK)
        p_own = p[..., top * MOBA_BLOCK:]
        return jnp.einsum('nqkj,nqkjd->nqd', p_sel, vs) + jnp.einsum('nqj,njd->nqd', p_own, vo)

    out = lax.map(one, (q_all, sel_all, val_all, jnp.arange(nq)))
    return out.transpose(1, 0, 2, 3).reshape(b, h, s_len, d)


def token_mixer(hn, w_in, b_gate, q_norm, w_uq, kv_norm, w_ukv, w_branch, w_out, cos_p, sin_p, cos_m, sin_m):
    bsz, s_len, _ = hn.shape
    proj = hn @ w_in
    moba_qkv, sb_qkv, c_q, c_kv, k_rope, gates = jnp.split(
        proj, [SPLIT_1, SPLIT_2, SPLIT_3, SPLIT_4, SPLIT_5], axis=-1)
    qa, ka, va = [to_heads(t, MOBA_HEADS) for t in jnp.split(moba_qkv, 3, axis=-1)]
    y_a = from_heads(moba_attention(partial_rope(qa, cos_p, sin_p), partial_rope(ka, cos_p, sin_p), va))
    qb_, kb_, vb_ = [to_heads(t, SB_HEADS) for t in jnp.split(sb_qkv, 3, axis=-1)]
    y_b = from_heads(stick_breaking_attention(qb_, kb_, vb_))
    qc = to_heads(rms_norm(c_q, q_norm) @ w_uq, MLA_HEADS)
    qc = jnp.concatenate([qc[..., :MLA_NOPE], apply_rope(qc[..., MLA_NOPE:], cos_m, sin_m)], axis=-1)
    kv = to_heads(rms_norm(c_kv, kv_norm) @ w_ukv, MLA_HEADS)
    k_nope, vc = kv[..., :MLA_NOPE], kv[..., MLA_NOPE:]
    kr = jnp.broadcast_to(apply_rope(k_rope, cos_m, sin_m)[:, None], (bsz, MLA_HEADS, s_len, MLA_ROPE))
    y_c = from_heads(causal_softmax_attention(qc, jnp.concatenate([k_nope, kr], axis=-1), vc))
    g = jax.nn.sigmoid(gates.reshape(bsz, s_len, N_BRANCH, D_MODEL) + b_gate)
    ys = jnp.stack([y_a, y_b, y_c], axis=2)
    merged = jnp.einsum('bsnk,nkd,bsnd->bsd', ys, w_branch, g)
    return merged @ w_out


def swiglu(h, w_gate, w_up, w_down):
    return (jax.nn.silu(h @ w_gate) * (h @ w_up)) @ w_down


def modulate(x, g_pre, shift, scale):
    return rms_norm(x, g_pre) * (1 + scale) + shift


def residual_add(x, y, g_post, gate, res_w):
    return x + res_w * (1 + gate) * rms_norm(y, g_post)


def setup_inputs(seed: int = 0) -> dict:
    key = jax.random.key(seed)
    ks = jax.random.split(key, 20)
    f32 = jnp.float32
    nrm = lambda k, shape, fan: jax.random.normal(k, shape, f32) * fan ** -0.5
    return {
        'x': jax.random.normal(ks[0], (BATCH, SEQ, D_MODEL), f32),
        'c': jax.random.normal(ks[1], (BATCH, D_MODEL), f32),
        'ada_w': 0.2 * nrm(ks[2], (DEPTH, D_MODEL, N_SUB * 3 * D_MODEL), D_MODEL),
        'ada_b': 0.01 * jax.random.normal(ks[3], (DEPTH, N_SUB * 3 * D_MODEL), f32),
        'norm_pre': 1.0 + 0.05 * jax.random.normal(ks[4], (DEPTH, N_SUB, D_MODEL), f32),
        'norm_post': 1.0 + 0.05 * jax.random.normal(ks[5], (DEPTH, N_SUB, D_MODEL), f32),
        'ffn_w_gate': nrm(ks[6], (DEPTH, 2, D_MODEL, D_FF), D_MODEL),
        'ffn_w_up': nrm(ks[7], (DEPTH, 2, D_MODEL, D_FF), D_MODEL),
        'ffn_w_down': nrm(ks[8], (DEPTH, 2, D_FF, D_MODEL), D_FF),
        'mix_w_in': nrm(ks[9], (DEPTH, D_MODEL, N_IN), D_MODEL),
        'mix_b_gate': 0.1 * jax.random.normal(ks[10], (DEPTH, N_BRANCH, D_MODEL), f32),
        'mla_q_norm': 1.0 + 0.05 * jax.random.normal(ks[11], (DEPTH, MLA_Q_LORA), f32),
        'mla_w_uq': nrm(ks[12], (DEPTH, MLA_Q_LORA, MLA_HEADS * MLA_QK), MLA_Q_LORA),
        'mla_kv_norm': 1.0 + 0.05 * jax.random.normal(ks[13], (DEPTH, MLA_KV_LORA), f32),
        'mla_w_ukv': nrm(ks[14], (DEPTH, MLA_KV_LORA, MLA_HEADS * (MLA_NOPE + MLA_V)), MLA_KV_LORA),
        'mix_w_branch': nrm(ks[15], (DEPTH, N_BRANCH, BRANCH_W, D_MODEL), BRANCH_W),
        'mix_w_out': nrm(ks[16], (DEPTH, D_MODEL, D_MODEL), D_MODEL),
    }


def reference(x, c, ada_w, ada_b, norm_pre, norm_post, ffn_w_gate, ffn_w_up, ffn_w_down, mix_w_in, mix_b_gate,
              mla_q_norm, mla_w_uq, mla_kv_norm, mla_w_ukv, mix_w_branch, mix_w_out):
    bsz, s_len, _ = x.shape
    cos_p, sin_p = rope_tables(PARTIAL_ROT, s_len)
    cos_m, sin_m = rope_tables(MLA_ROPE, s_len)
    c_act = jax.nn.silu(c)
    for l in range(DEPTH):
        mod = (c_act @ ada_w[l] + ada_b[l]).reshape(bsz, N_SUB, 3, D_MODEL)[:, :, :, None, :]
        h = modulate(x, norm_pre[l, 0], mod[:, 0, 0], mod[:, 0, 1])
        x = residual_add(x, swiglu(h, ffn_w_gate[l, 0], ffn_w_up[l, 0], ffn_w_down[l, 0]), norm_post[l, 0], mod[:, 0, 2], 0.5)
        h = modulate(x, norm_pre[l, 1], mod[:, 1, 0], mod[:, 1, 1])
        y = token_mixer(h, mix_w_in[l], mix_b_gate[l], mla_q_norm[l], mla_w_uq[l], mla_kv_norm[l], mla_w_ukv[l],
                        mix_w_branch[l], mix_w_out[l], cos_p, sin_p, cos_m, sin_m)
        x = residual_add(x, y, norm_post[l, 1], mod[:, 1, 2], 1.0)
        h = modulate(x, norm_pre[l, 2], mod[:, 2, 0], mod[:, 2, 1])
        x = residual_add(x, swiglu(h, ffn_w_gate[l, 1], ffn_w_up[l, 1], ffn_w_down[l, 1]), norm_post[l, 2], mod[:, 2, 2], 0.5)
    return x
```

```python
import functools

import jax
import jax.numpy as jnp
from jax import lax
from jax.experimental import pallas as pl
from jax.experimental.pallas import tpu as pltpu

F32 = jnp.float32
BF16 = jnp.bfloat16

D_MODEL = 1024
N_HEADS = 8
HEAD_DIM = 64
MOBA_BLOCK = 256
MOBA_TOPK = 3
MLA_Q_LORA = 256
MLA_KV_LORA = 128
MLA_NOPE = 64
MLA_ROPE = 32
MLA_V = 64
MLA_QK = MLA_NOPE + MLA_ROPE
ROPE_THETA = 500000.0
PARTIAL_ROT = HEAD_DIM // 4
D_FF = 2816
N_BRANCH = 3
BRANCH_W = N_HEADS * HEAD_DIM
N_SUB = 3
EPS = 1e-6

LANES = 128
V7X_VMEM_LIMIT = 56 * 1024 * 1024

SEQ_TILE = MOBA_BLOCK
FFN_TILE = 512
MASKED = -1e30
HEAD_PAD = LANES


def _dot(a, b):
    return jnp.dot(a, b, preferred_element_type=F32)


def _dot_nt(a, b, precision=None):
    return lax.dot_general(a, b, (((1,), (1,)), ((), ())), precision=precision, preferred_element_type=F32)


def _rms_norm(x, g):
    return x * lax.rsqrt(jnp.mean(x * x, axis=-1, keepdims=True) + EPS) * g


def _const_spec(shape):
    nd = len(shape)
    return pl.BlockSpec(shape, lambda *_: (0,) * nd, pipeline_mode=pl.Buffered(1))


def _params(n_grid):
    return pltpu.CompilerParams(dimension_semantics=("arbitrary",) * n_grid, vmem_limit_bytes=V7X_VMEM_LIMIT)


def _mod_kernel(c_ref, w_ref, b_ref, o_ref):
    c = c_ref[...]
    ca = c * jax.nn.sigmoid(c)
    o_ref[0] = jnp.dot(ca, w_ref[0], precision=lax.Precision.HIGHEST, preferred_element_type=F32) + b_ref[0]


def _modulation(c, ada_w, ada_b):
    depth, d, n = ada_w.shape
    tn = n // 8
    c8 = jnp.broadcast_to(c.reshape(1, d), (8, d))
    out = pl.pallas_call(
        _mod_kernel,
        grid=(depth, n // tn),
        in_specs=[
            pl.BlockSpec((8, d), lambda l, j: (0, 0)),
            pl.BlockSpec((1, d, tn), lambda l, j: (l, 0, j)),
            pl.BlockSpec((1, 1, tn), lambda l, j: (l, 0, j)),
        ],
        out_specs=pl.BlockSpec((1, 8, tn), lambda l, j: (l, 0, j)),
        out_shape=jax.ShapeDtypeStruct((depth, 8, n), F32),
        compiler_params=_params(2),
        name="adaln_mod",
    )(c8, ada_w, ada_b.reshape(depth, 1, n))
    return out[:, 0, :].reshape(depth, N_SUB, 3, d)


def _ffn_kernel(x_ref, mod_ref, gpre_ref, gpost_ref, wg_ref, wu_ref, wd_ref, o_ref, *, res_w):
    x = x_ref[...]
    shift, scale, gate = mod_ref[0:1, :], mod_ref[1:2, :], mod_ref[2:3, :]
    h = (_rms_norm(x, gpre_ref[...]) * (1.0 + scale) + shift).astype(BF16)
    g = _dot(h, wg_ref[...])
    u = _dot(h, wu_ref[...])
    a = (g * jax.nn.sigmoid(g) * u).astype(BF16)
    y = _dot(a, wd_ref[...])
    o_ref[...] = x + (res_w * (1.0 + gate)) * _rms_norm(y, gpost_ref[...])


def _ffn(x, mod, g_pre, g_post, wg, wu, wd, res_w):
    s, d = x.shape
    tm = min(FFN_TILE, s)
    dff = wg.shape[1]
    return pl.pallas_call(
        functools.partial(_ffn_kernel, res_w=res_w),
        grid=(s // tm,),
        in_specs=[
            pl.BlockSpec((tm, d), lambda i: (i, 0)),
            _const_spec((3, d)),
            _const_spec((1, d)),
            _const_spec((1, d)),
            _const_spec((d, dff)),
            _const_spec((d, dff)),
            _const_spec((dff, d)),
        ],
        out_specs=pl.BlockSpec((tm, d), lambda i: (i, 0)),
        out_shape=jax.ShapeDtypeStruct((s, d), F32),
        compiler_params=_params(1),
        name="macaron_ffn",
    )(x, mod, g_pre.reshape(1, d), g_post.reshape(1, d), wg, wu, wd)


def _rope_rows(x, a, b, c, half):
    outs = []
    for g in range(x.shape[1] // LANES):
        xg = x[:, g * LANES:(g + 1) * LANES]
        outs.append(xg * a + pltpu.roll(xg, LANES - half, 1) * b + pltpu.roll(xg, half, 1) * c)
    return jnp.concatenate(outs, axis=1) if len(outs) > 1 else outs[0]


def _proj_kernel(x_ref, mod_ref, gpre_ref, wrow_ref, wt_ref, wuq_ref, wuk_ref, wuvt_ref, qn_ref, kvn_ref, bg_ref,
                 pa_ref, pb_ref, pc_ref, ma_ref, mb_ref, mc_ref,
                 qa_ref, ka_ref, vat_ref, sel_ref, qb_ref, kb_ref, vbt_ref, qc_ref, kc_ref, vct_ref, gt_ref,
                 kmean_ref, *, n_blocks):
    i = pl.program_id(0)
    w = BRANCH_W
    x = x_ref[...]
    shift, scale = mod_ref[0:1, :], mod_ref[1:2, :]
    hn = (_rms_norm(x, gpre_ref[...]) * (1.0 + scale) + shift).astype(BF16)
    rows = _dot(hn, wrow_ref[...])
    cols_t = _dot_nt(wt_ref[...], hn)

    pa, pb, pc = pa_ref[...], pb_ref[...], pc_ref[...]
    q_a = _rope_rows(rows[:, 0:w], pa, pb, pc, PARTIAL_ROT // 2)
    k_a = _rope_rows(rows[:, w:2 * w], pa, pb, pc, PARTIAL_ROT // 2)
    qa_ref[...] = (q_a * HEAD_DIM ** -0.5).astype(BF16)
    ka_ref[...] = k_a.astype(BF16)
    vat_ref[0] = cols_t[0:w, :].astype(BF16)

    @pl.when(i == 0)
    def _():
        kmean_ref[...] = jnp.zeros_like(kmean_ref)

    kmean_ref[pl.ds(i, 1), :] = jnp.mean(k_a, axis=0, keepdims=True)
    kmean = kmean_ref[...]
    t = x.shape[0]
    lane = lax.broadcasted_iota(jnp.int32, (1, LANES), 1)
    blk = lax.broadcasted_iota(jnp.int32, (n_blocks, t), 0)
    past = blk < i
    blk_f = blk.astype(F32)
    for h in range(N_HEADS):
        p, e = h // 2, h % 2
        in_head = (lane >= HEAD_DIM) if e else (lane < HEAD_DIM)
        qm = jnp.where(in_head, q_a[:, p * LANES:(p + 1) * LANES], 0.0)
        gate = _dot_nt(kmean[:, p * LANES:(p + 1) * LANES], qm, precision=lax.Precision.HIGHEST)
        gate = jnp.where(past, gate, -jnp.inf)
        chosen = jnp.zeros(gate.shape, F32)
        for _ in range(min(MOBA_TOPK, n_blocks)):
            top = jnp.max(gate, axis=0, keepdims=True)
            first = jnp.min(jnp.where(gate == top, blk_f, float(n_blocks)), axis=0, keepdims=True)
            pick = blk_f == first
            chosen = jnp.where(pick, 1.0, chosen)
            gate = jnp.where(pick, -jnp.inf, gate)
        sel_ref[h] = jnp.where(jnp.logical_and(chosen > 0.0, past), 0.0, MASKED)

    qb_ref[...] = (rows[:, 2 * w:3 * w] * HEAD_DIM ** -0.5).astype(BF16)
    kb_ref[...] = rows[:, 3 * w:4 * w].astype(BF16)
    vbt_ref[0] = cols_t[w:2 * w, :].astype(BF16)

    o = 4 * w
    ma, mb, mc = ma_ref[...], mb_ref[...], mc_ref[...]
    cq = _rms_norm(rows[:, o:o + MLA_Q_LORA], qn_ref[...]).astype(BF16)
    q_c = _rope_rows(_dot(cq, wuq_ref[...]), ma, mb, mc, MLA_ROPE // 2)
    qc_ref[...] = (q_c * MLA_QK ** -0.5).astype(BF16)
    o += MLA_Q_LORA
    ckv = _rms_norm(rows[:, o:o + MLA_KV_LORA], kvn_ref[...]).astype(BF16)
    o += MLA_KV_LORA
    k_rope = _rope_rows(rows[:, o:o + HEAD_PAD], ma, mb, mc, MLA_ROPE // 2)
    kc_ref[...] = (_dot(ckv, wuk_ref[...]) + jnp.concatenate([k_rope] * N_HEADS, axis=1)).astype(BF16)
    vct_ref[0] = _dot_nt(wuvt_ref[...], ckv).astype(BF16)

    gt_ref[...] = jax.nn.sigmoid(cols_t[2 * w:, :] + bg_ref[...]).astype(BF16)


def _mixer_proj(x, mod, g_pre, wp, tabs):
    s, d = x.shape
    t = SEQ_TILE
    nb = s // t
    w = BRANCH_W
    hp = N_HEADS * HEAD_PAD
    row_spec = lambda n: pl.BlockSpec((t, n), lambda i: (i, 0))
    vt_spec = pl.BlockSpec((1, w, t), lambda i: (i, 0, 0))
    tab_spec = pl.BlockSpec((t, LANES), lambda i: (i, 0))
    out_shape = (
        jax.ShapeDtypeStruct((s, w), BF16), jax.ShapeDtypeStruct((s, w), BF16), jax.ShapeDtypeStruct((nb, w, t), BF16),
        jax.ShapeDtypeStruct((N_HEADS, nb, s), F32),
        jax.ShapeDtypeStruct((s, w), BF16), jax.ShapeDtypeStruct((s, w), BF16), jax.ShapeDtypeStruct((nb, w, t), BF16),
        jax.ShapeDtypeStruct((s, hp), BF16), jax.ShapeDtypeStruct((s, hp), BF16), jax.ShapeDtypeStruct((nb, w, t), BF16),
        jax.ShapeDtypeStruct((N_BRANCH * d, s), BF16),
    )
    out_specs = (
        row_spec(w), row_spec(w), vt_spec,
        pl.BlockSpec((N_HEADS, nb, t), lambda i: (0, 0, i)),
        row_spec(w), row_spec(w), vt_spec,
        row_spec(hp), row_spec(hp), vt_spec,
        pl.BlockSpec((N_BRANCH * d, t), lambda i: (0, i)),
    )
    in_specs = [
        pl.BlockSpec((t, d), lambda i: (i, 0)),
        _const_spec((3, d)), _const_spec((1, d)),
        _const_spec(wp["w_row"].shape), _const_spec(wp["w_t"].shape), _const_spec(wp["w_uq"].shape),
        _const_spec(wp["w_uk"].shape), _const_spec(wp["w_uvt"].shape),
        _const_spec((1, MLA_Q_LORA)), _const_spec((1, MLA_KV_LORA)), _const_spec((N_BRANCH * d, 1)),
    ] + [tab_spec] * 6
    return pl.pallas_call(
        functools.partial(_proj_kernel, n_blocks=nb),
        grid=(nb,),
        in_specs=in_specs,
        out_specs=out_specs,
        out_shape=out_shape,
        scratch_shapes=[pltpu.VMEM((nb, w), F32)],
        compiler_params=_params(1),
        name="mixer_proj",
    )(x, mod, g_pre.reshape(1, d), wp["w_row"], wp["w_t"], wp["w_uq"], wp["w_uk"], wp["w_uvt"],
      wp["q_norm"], wp["kv_norm"], wp["b_gate"], *tabs)


def _head_lanes(q, e):
    lane = lax.broadcasted_iota(jnp.int32, (1, LANES), 1)
    keep = lax.shift_right_logical(lane, HEAD_DIM.bit_length() - 1) == e
    return jnp.where(keep, q, jnp.zeros_like(q))


def _key_tile(k_ref, j):
    t = SEQ_TILE
    return k_ref[pl.ds(pl.multiple_of(j * t, t), t), :]


def _softmax_step(carry, s, vt):
    m, l, acc = carry
    m_new = jnp.maximum(m, jnp.max(s, axis=0, keepdims=True))
    p = jnp.exp(s - m_new)
    alpha = jnp.exp(m - m_new)
    l = alpha * l + jnp.sum(p, axis=0, keepdims=True)
    acc = alpha * acc + _dot(vt, p.astype(BF16))
    return m_new, l, acc


def _softmax_init(t):
    return (jnp.full((1, t), MASKED, F32), jnp.zeros((1, t), F32), jnp.zeros((HEAD_DIM, t), F32))


def _causal_tile_mask(t, strict):
    key = lax.broadcasted_iota(jnp.int32, (t, t), 0)
    qry = lax.broadcasted_iota(jnp.int32, (t, t), 1)
    return (key < qry) if strict else (key <= qry)


def _moba_kernel(q_ref, k_ref, vt_ref, sel_ref, o_ref):
    h, i = pl.program_id(0), pl.program_id(1)
    t = SEQ_TILE
    qm = _head_lanes(q_ref[...], h % 2)
    s = jnp.where(_causal_tile_mask(t, False), _dot_nt(_key_tile(k_ref, i), qm), MASKED)
    carry = _softmax_step(_softmax_init(t), s, vt_ref[i])

    def past_block(j, carry):
        s = _dot_nt(_key_tile(k_ref, j), qm) + sel_ref[0, pl.ds(j, 1), :]
        return _softmax_step(carry, s, vt_ref[j])

    _, l, acc = lax.fori_loop(0, i, past_block, carry)
    o_ref[...] = (acc / l).astype(o_ref.dtype)


def _mla_kernel(q_ref, k_ref, vt_ref, o_ref):
    i = pl.program_id(1)
    t = SEQ_TILE
    q = q_ref[...]
    s = jnp.where(_causal_tile_mask(t, False), _dot_nt(_key_tile(k_ref, i), q), MASKED)
    carry = _softmax_step(_softmax_init(t), s, vt_ref[i])

    def past_tile(j, carry):
        return _softmax_step(carry, _dot_nt(_key_tile(k_ref, j), q), vt_ref[j])

    _, l, acc = lax.fori_loop(0, i, past_tile, carry)
    o_ref[...] = (acc / l).astype(o_ref.dtype)


def _log1m_beta(z):
    return -(jnp.maximum(z, 0.0) + jnp.log1p(jnp.exp(-jnp.abs(z))))


def _sb_step(carry, z, log1m, keep, vt, upper):
    c, acc = carry
    hi = log1m.astype(BF16)
    lo = (log1m - hi.astype(F32)).astype(BF16)
    suffix = _dot(upper, hi) + _dot(upper, lo) + c
    wgt = jnp.exp(z + log1m + suffix)
    if keep is not None:
        wgt = jnp.where(keep, wgt, 0.0)
    acc = acc + _dot(vt, wgt.astype(BF16))
    c = c + jnp.sum(log1m, axis=0, keepdims=True)
    return c, acc


def _sb_kernel(q_ref, k_ref, vt_ref, up_ref, o_ref):
    h, i = pl.program_id(0), pl.program_id(1)
    t = SEQ_TILE
    qm = _head_lanes(q_ref[...], h % 2)
    upper = up_ref[...]
    keep = _causal_tile_mask(t, True)
    z = _dot_nt(_key_tile(k_ref, i), qm)
    log1m = jnp.where(keep, _log1m_beta(z), 0.0)
    carry = _sb_step((jnp.zeros((1, t), F32), jnp.zeros((HEAD_DIM, t), F32)), z, log1m, keep, vt_ref[i], upper)

    def past_tile(n, carry):
        j = i - 1 - n
        z = _dot_nt(_key_tile(k_ref, j), qm)
        return _sb_step(carry, z, _log1m_beta(z), None, vt_ref[j], upper)

    _, acc = lax.fori_loop(0, i, past_tile, carry)
    o_ref[...] = acc.astype(o_ref.dtype)


def _attention(kernel, q, k, vt, extra, extra_specs, heads_per_group, name):
    s = q.shape[0]
    t = SEQ_TILE
    nb = s // t
    g = heads_per_group
    return pl.pallas_call(
        kernel,
        grid=(N_HEADS, nb),
        in_specs=[
            pl.BlockSpec((t, LANES), lambda h, i: (i, h // g)),
            pl.BlockSpec((s, LANES), lambda h, i: (0, h // g)),
            pl.BlockSpec((nb, HEAD_DIM, t), lambda h, i: (0, h, 0)),
        ] + extra_specs,
        out_specs=pl.BlockSpec((HEAD_DIM, t), lambda h, i: (h, i)),
        out_shape=jax.ShapeDtypeStruct((N_HEADS * HEAD_DIM, s), BF16),
        compiler_params=_params(2),
        name=name,
    )(q, k, vt, *extra)


def _merge_kernel(x_ref, ya_ref, yb_ref, yc_ref, gt_ref, wbt_ref, wot_ref, mod_ref, gpost_ref, o_ref):
    d = D_MODEL
    merged_t = None
    for n, y_ref in enumerate((ya_ref, yb_ref, yc_ref)):
        part = _dot(wbt_ref[n], y_ref[...]) * gt_ref[n * d:(n + 1) * d, :].astype(F32)
        merged_t = part if merged_t is None else merged_t + part
    out = _dot(wot_ref[...], merged_t.astype(BF16)).T
    gate = mod_ref[2:3, :]
    o_ref[...] = x_ref[...] + (1.0 + gate) * _rms_norm(out, gpost_ref[...])


def _merge(x, ya, yb, yc, gt, wbt, wot, mod, g_post):
    s, d = x.shape
    t = SEQ_TILE
    y_spec = pl.BlockSpec((BRANCH_W, t), lambda i: (0, i))
    return pl.pallas_call(
        _merge_kernel,
        grid=(s // t,),
        in_specs=[
            pl.BlockSpec((t, d), lambda i: (i, 0)),
            y_spec, y_spec, y_spec,
            pl.BlockSpec((N_BRANCH * d, t), lambda i: (0, i)),
            _const_spec(wbt.shape), _const_spec(wot.shape), _const_spec((3, d)), _const_spec((1, d)),
        ],
        out_specs=pl.BlockSpec((t, d), lambda i: (i, 0)),
        out_shape=jax.ShapeDtypeStruct((s, d), F32),
        compiler_params=_params(1),
        name="mixer_merge",
    )(x, ya, yb, yc, gt, wbt, wot, mod, g_post.reshape(1, d))


def _rope_tables(n_rot, seq, first_lane, period):
    half = n_rot // 2
    inv = jnp.power(jnp.float32(ROPE_THETA), -jnp.arange(0, n_rot, 2, dtype=F32) / n_rot)
    ang = jnp.arange(seq, dtype=F32)[:, None] * inv[None, :]
    cos, sin = jnp.cos(ang), jnp.sin(ang)
    lane = jnp.arange(LANES) % period - first_lane
    lo = (lane >= 0) & (lane < half)
    hi = (lane >= half) & (lane < n_rot)
    idx = jnp.clip(lane, 0, n_rot - 1) % half
    a = jnp.where((lo | hi)[None, :], cos[:, idx], 1.0)
    b = jnp.where(lo[None, :], -sin[:, idx], 0.0)
    c = jnp.where(hi[None, :], sin[:, idx], 0.0)
    return a, b, c


def _pad_heads(w, n_heads, width):
    k = w.shape[0]
    w = w.reshape(k, n_heads, width)
    return jnp.pad(w, ((0, 0), (0, 0), (0, HEAD_PAD - width))).reshape(k, n_heads * HEAD_PAD)


def _mixer_weights(w_in, b_gate, q_norm, w_uq, kv_norm, w_ukv, w_branch, w_out):
    w = BRANCH_W
    d = D_MODEL
    qa, ka, va, qb, kb, vb = (w_in[:, n * w:(n + 1) * w] for n in range(6))
    o = 6 * w
    w_cq = w_in[:, o:o + MLA_Q_LORA]
    o += MLA_Q_LORA
    w_ckv = w_in[:, o:o + MLA_KV_LORA]
    o += MLA_KV_LORA
    w_kr = jnp.pad(w_in[:, o:o + MLA_ROPE], ((0, 0), (MLA_NOPE, HEAD_PAD - MLA_QK)))
    o += MLA_ROPE
    w_gates = w_in[:, o:]
    ukv = w_ukv.reshape(MLA_KV_LORA, N_HEADS, MLA_NOPE + MLA_V)
    return {
        "w_row": jnp.concatenate([qa, ka, qb, kb, w_cq, w_ckv, w_kr], axis=1).astype(BF16),
        "w_t": jnp.concatenate([va, vb, w_gates], axis=1).T.astype(BF16),
        "w_uq": _pad_heads(w_uq, N_HEADS, MLA_QK).astype(BF16),
        "w_uk": _pad_heads(ukv[:, :, :MLA_NOPE].reshape(MLA_KV_LORA, -1), N_HEADS, MLA_NOPE).astype(BF16),
        "w_uvt": ukv[:, :, MLA_NOPE:].reshape(MLA_KV_LORA, -1).T.astype(BF16),
        "q_norm": q_norm.reshape(1, -1),
        "kv_norm": kv_norm.reshape(1, -1),
        "b_gate": b_gate.reshape(N_BRANCH * d, 1),
        "w_bt": jnp.swapaxes(w_branch, 1, 2).astype(BF16),
        "w_ot": w_out.T.astype(BF16),
    }


def kernel(x, c, ada_w, ada_b, norm_pre, norm_post, ffn_w_gate, ffn_w_up, ffn_w_down, mix_w_in, mix_b_gate,
           mla_q_norm, mla_w_uq, mla_kv_norm, mla_w_ukv, mix_w_branch, mix_w_out):
    bsz, s, d = x.shape
    assert bsz == 1 and d == D_MODEL and s % SEQ_TILE == 0 and s % min(FFN_TILE, s) == 0
    depth = ada_w.shape[0]
    t = SEQ_TILE
    mod = _modulation(c, ada_w, ada_b)
    tabs = _rope_tables(PARTIAL_ROT, s, 0, HEAD_DIM) + _rope_tables(MLA_ROPE, s, MLA_NOPE, HEAD_PAD)
    upper = (jnp.arange(t)[None, :] > jnp.arange(t)[:, None]).astype(BF16)
    xs = x.reshape(s, d)
    for l in range(depth):
        xs = _ffn(xs, mod[l, 0], norm_pre[l, 0], norm_post[l, 0], ffn_w_gate[l, 0].astype(BF16),
                  ffn_w_up[l, 0].astype(BF16), ffn_w_down[l, 0].astype(BF16), 0.5)
        wp = _mixer_weights(mix_w_in[l], mix_b_gate[l], mla_q_norm[l], mla_w_uq[l], mla_kv_norm[l], mla_w_ukv[l],
                            mix_w_branch[l], mix_w_out[l])
        qa, ka, vat, sel, qb, kb, vbt, qc, kc, vct, gt = _mixer_proj(xs, mod[l, 1], norm_pre[l, 1], wp, tabs)
        ya = _attention(_moba_kernel, qa, ka, vat, (sel,),
                        [pl.BlockSpec((1, s // t, t), lambda h, i: (h, 0, i))], 2, "moba_attention")
        yb = _attention(_sb_kernel, qb, kb, vbt, (upper,), [_const_spec((t, t))], 2, "stick_breaking_attention")
        yc = _attention(_mla_kernel, qc, kc, vct, (), [], 1, "mla_attention")
        xs = _merge(xs, ya, yb, yc, gt, wp["w_bt"], wp["w_ot"], mod[l, 1], norm_post[l, 1])
        xs = _ffn(xs, mod[l, 2], norm_pre[l, 2], norm_post[l, 2], ffn_w_gate[l, 1].astype(BF16),
                  ffn_w_up[l, 1].astype(BF16), ffn_w_down[l, 1].astype(BF16), 0.5)
    return xs.reshape(bsz, s, d)
```

```python
import functools

import jax
import jax.numpy as jnp
from jax import lax
from jax.experimental import pallas as pl
from jax.experimental.pallas import tpu as pltpu

F32 = jnp.float32
BF16 = jnp.bfloat16

D_MODEL = 1024
N_HEADS = 8
HEAD_DIM = 64
MOBA_BLOCK = 256
MOBA_TOPK = 3
MLA_Q_LORA = 256
MLA_KV_LORA = 128
MLA_NOPE = 64
MLA_ROPE = 32
MLA_V = 64
MLA_QK = MLA_NOPE + MLA_ROPE
ROPE_THETA = 500000.0
PARTIAL_ROT = HEAD_DIM // 4
D_FF = 2816
N_BRANCH = 3
BRANCH_W = N_HEADS * HEAD_DIM
N_SUB = 3
EPS = 1e-6

LANES = 128
V7X_VMEM_LIMIT = 56 * 1024 * 1024

SEQ_TILE = MOBA_BLOCK
FFN_TILE = 512
MASKED = -1e30
HEAD_PAD = LANES


def _dot(a, b):
    return jnp.dot(a, b, preferred_element_type=F32)


def _dot_nt(a, b, precision=None):
    return lax.dot_general(a, b, (((1,), (1,)), ((), ())), precision=precision, preferred_element_type=F32)


def _rms_norm(x, g):
    return x * lax.rsqrt(jnp.mean(x * x, axis=-1, keepdims=True) + EPS) * g


def _const_spec(shape):
    nd = len(shape)
    return pl.BlockSpec(shape, lambda *_: (0,) * nd, pipeline_mode=pl.Buffered(1))


def _params(n_grid):
    return pltpu.CompilerParams(dimension_semantics=("arbitrary",) * n_grid, vmem_limit_bytes=V7X_VMEM_LIMIT)


def _mod_kernel(c_ref, w_ref, b_ref, o_ref):
    c = c_ref[...]
    ca = c * jax.nn.sigmoid(c)
    o_ref[0] = jnp.dot(ca, w_ref[0], precision=lax.Precision.HIGHEST, preferred_element_type=F32) + b_ref[0]


def _modulation(c, ada_w, ada_b):
    depth, d, n = ada_w.shape
    tn = n // 8
    c8 = jnp.broadcast_to(c.reshape(1, d), (8, d))
    out = pl.pallas_call(
        _mod_kernel,
        grid=(depth, n // tn),
        in_specs=[
            pl.BlockSpec((8, d), lambda l, j: (0, 0)),
            pl.BlockSpec((1, d, tn), lambda l, j: (l, 0, j)),
            pl.BlockSpec((1, 1, tn), lambda l, j: (l, 0, j)),
        ],
        out_specs=pl.BlockSpec((1, 8, tn), lambda l, j: (l, 0, j)),
        out_shape=jax.ShapeDtypeStruct((depth, 8, n), F32),
        compiler_params=_params(2),
        name="adaln_mod",
    )(c8, ada_w, ada_b.reshape(depth, 1, n))
    return out[:, 0, :].reshape(depth, N_SUB, 3, d)


def _ffn_kernel(x_ref, mod_ref, gpre_ref, gpost_ref, wg_ref, wu_ref, wd_ref, o_ref, *, res_w):
    x = x_ref[...]
    shift, scale, gate = mod_ref[0:1, :], mod_ref[1:2, :], mod_ref[2:3, :]
    h = (_rms_norm(x, gpre_ref[...]) * (1.0 + scale) + shift).astype(BF16)
    g = _dot(h, wg_ref[...])
    u = _dot(h, wu_ref[...])
    a = (g * jax.nn.sigmoid(g) * u).astype(BF16)
    y = _dot(a, wd_ref[...])
    o_ref[...] = x + (res_w * (1.0 + gate)) * _rms_norm(y, gpost_ref[...])


def _ffn(x, mod, g_pre, g_post, wg, wu, wd, res_w):
    s, d = x.shape
    tm = min(FFN_TILE, s)
    dff = wg.shape[1]
    return pl.pallas_call(
        functools.partial(_ffn_kernel, res_w=res_w),
        grid=(s // tm,),
        in_specs=[
            pl.BlockSpec((tm, d), lambda i: (i, 0)),
            _const_spec((3, d)),
            _const_spec((1, d)),
            _const_spec((1, d)),
            _const_spec((d, dff)),
            _const_spec((d, dff)),
            _const_spec((dff, d)),
        ],
        out_specs=pl.BlockSpec((tm, d), lambda i: (i, 0)),
        out_shape=jax.ShapeDtypeStruct((s, d), F32),
        compiler_params=_params(1),
        name="macaron_ffn",
    )(x, mod, g_pre.reshape(1, d), g_post.reshape(1, d), wg, wu, wd)


def _rope_rows(x, a, b, c, half):
    outs = []
    for g in range(x.shape[1] // LANES):
        xg = x[:, g * LANES:(g + 1) * LANES]
        outs.append(xg * a + pltpu.roll(xg, LANES - half, 1) * b + pltpu.roll(xg, half, 1) * c)
    return jnp.concatenate(outs, axis=1) if len(outs) > 1 else outs[0]


def _proj_kernel(x_ref, mod_ref, gpre_ref, wrow_ref, wt_ref, wuq_ref, wuk_ref, wuvt_ref, qn_ref, kvn_ref, bg_ref,
                 pa_ref, pb_ref, pc_ref, ma_ref, mb_ref, mc_ref,
                 qa_ref, ka_ref, vat_ref, sel_ref, qb_ref, kb_ref, vbt_ref, qc_ref, kc_ref, vct_ref, gt_ref,
                 kmean_ref, *, n_blocks):
    i = pl.program_id(0)
    w = BRANCH_W
    x = x_ref[...]
    shift, scale = mod_ref[0:1, :], mod_ref[1:2, :]
    hn = (_rms_norm(x, gpre_ref[...]) * (1.0 + scale) + shift).astype(BF16)
    rows = _dot(hn, wrow_ref[...])
    cols_t = _dot_nt(wt_ref[...], hn)

    pa, pb, pc = pa_ref[...], pb_ref[...], pc_ref[...]
    q_a = _rope_rows(rows[:, 0:w], pa, pb, pc, PARTIAL_ROT // 2)
    k_a = _rope_rows(rows[:, w:2 * w], pa, pb, pc, PARTIAL_ROT // 2)
    qa_ref[...] = (q_a * HEAD_DIM ** -0.5).astype(BF16)
    ka_ref[...] = k_a.astype(BF16)
    vat_ref[0] = cols_t[0:w, :].astype(BF16)

    @pl.when(i == 0)
    def _():
        kmean_ref[...] = jnp.zeros_like(kmean_ref)

    kmean_ref[pl.ds(i, 1), :] = jnp.mean(k_a, axis=0, keepdims=True)
    kmean = kmean_ref[...]
    t = x.shape[0]
    lane = lax.broadcasted_iota(jnp.int32, (1, LANES), 1)
    blk = lax.broadcasted_iota(jnp.int32, (n_blocks, t), 0)
    past = blk < i
    blk_f = blk.astype(F32)
    for h in range(N_HEADS):
        p, e = h // 2, h % 2
        in_head = (lane >= HEAD_DIM) if e else (lane < HEAD_DIM)
        qm = jnp.where(in_head, q_a[:, p * LANES:(p + 1) * LANES], 0.0)
        gate = _dot_nt(kmean[:, p * LANES:(p + 1) * LANES], qm, precision=lax.Precision.HIGHEST)
        gate = jnp.where(past, gate, -jnp.inf)
        chosen = jnp.zeros(gate.shape, F32)
        for _ in range(min(MOBA_TOPK, n_blocks)):
            top = jnp.max(gate, axis=0, keepdims=True)
            first = jnp.min(jnp.where(gate == top, blk_f, float(n_blocks)), axis=0, keepdims=True)
            pick = blk_f == first
            chosen = jnp.where(pick, 1.0, chosen)
            gate = jnp.where(pick, -jnp.inf, gate)
        sel_ref[h] = jnp.where(jnp.logical_and(chosen > 0.0, past), 0.0, MASKED)

    qb_ref[...] = (rows[:, 2 * w:3 * w] * HEAD_DIM ** -0.5).astype(BF16)
    kb_ref[...] = rows[:, 3 * w:4 * w].astype(BF16)
    vbt_ref[0] = cols_t[w:2 * w, :].astype(BF16)

    o = 4 * w
    ma, mb, mc = ma_ref[...], mb_ref[...], mc_ref[...]
    cq = _rms_norm(rows[:, o:o + MLA_Q_LORA], qn_ref[...]).astype(BF16)
    q_c = _rope_rows(_dot(cq, wuq_ref[...]), ma, mb, mc, MLA_ROPE // 2)
    qc_ref[...] = (q_c * MLA_QK ** -0.5).astype(BF16)
    o += MLA_Q_LORA
    ckv = _rms_norm(rows[:, o:o + MLA_KV_LORA], kvn_ref[...]).astype(BF16)
    o += MLA_KV_LORA
    k_rope = _rope_rows(rows[:, o:o + HEAD_PAD], ma, mb, mc, MLA_ROPE // 2)
    kc_ref[...] = (_dot(ckv, wuk_ref[...]) + jnp.concatenate([k_rope] * N_HEADS, axis=1)).astype(BF16)
    vct_ref[0] = _dot_nt(wuvt_ref[...], ckv).astype(BF16)

    gt_ref[...] = jax.nn.sigmoid(cols_t[2 * w:, :] + bg_ref[...]).astype(BF16)


def _mixer_proj(x, mod, g_pre, wp, tabs):
    s, d = x.shape
    t = SEQ_TILE
    nb = s // t
    w = BRANCH_W
    hp = N_HEADS * HEAD_PAD
    row_spec = lambda n: pl.BlockSpec((t, n), lambda i: (i, 0))
    vt_spec = pl.BlockSpec((1, w, t), lambda i: (i, 0, 0))
    tab_spec = pl.BlockSpec((t, LANES), lambda i: (i, 0))
    out_shape = (
        jax.ShapeDtypeStruct((s, w), BF16), jax.ShapeDtypeStruct((s, w), BF16), jax.ShapeDtypeStruct((nb, w, t), BF16),
        jax.ShapeDtypeStruct((N_HEADS, nb, s), F32),
        jax.ShapeDtypeStruct((s, w), BF16), jax.ShapeDtypeStruct((s, w), BF16), jax.ShapeDtypeStruct((nb, w, t), BF16),
        jax.ShapeDtypeStruct((s, hp), BF16), jax.ShapeDtypeStruct((s, hp), BF16), jax.ShapeDtypeStruct((nb, w, t), BF16),
        jax.ShapeDtypeStruct((N_BRANCH * d, s), BF16),
    )
    out_specs = (
        row_spec(w), row_spec(w), vt_spec,
        pl.BlockSpec((N_HEADS, nb, t), lambda i: (0, 0, i)),
        row_spec(w), row_spec(w), vt_spec,
        row_spec(hp), row_spec(hp), vt_spec,
        pl.BlockSpec((N_BRANCH * d, t), lambda i: (0, i)),
    )
    in_specs = [
        pl.BlockSpec((t, d), lambda i: (i, 0)),
        _const_spec((3, d)), _const_spec((1, d)),
        _const_spec(wp["w_row"].shape), _const_spec(wp["w_t"].shape), _const_spec(wp["w_uq"].shape),
        _const_spec(wp["w_uk"].shape), _const_spec(wp["w_uvt"].shape),
        _const_spec((1, MLA_Q_LORA)), _const_spec((1, MLA_KV_LORA)), _const_spec((N_BRANCH * d, 1)),
    ] + [tab_spec] * 6
    return pl.pallas_call(
        functools.partial(_proj_kernel, n_blocks=nb),
        grid=(nb,),
        in_specs=in_specs,
        out_specs=out_specs,
        out_shape=out_shape,
        scratch_shapes=[pltpu.VMEM((nb, w), F32)],
        compiler_params=_params(1),
        name="mixer_proj",
    )(x, mod, g_pre.reshape(1, d), wp["w_row"], wp["w_t"], wp["w_uq"], wp["w_uk"], wp["w_uvt"],
      wp["q_norm"], wp["kv_norm"], wp["b_gate"], *tabs)


ATT_TILE = 512
CHUNKS = ATT_TILE // SEQ_TILE
PAIR = 2


def _key_rows(c):
    return pl.ds(pl.multiple_of(c * SEQ_TILE, SEQ_TILE), SEQ_TILE)


def _packed_pair(q_ref, k_ref):
    q = q_ref[...]
    lane = lax.broadcasted_iota(jnp.int32, (1, LANES), 1)
    qs = [jnp.where((lane >= HEAD_DIM) if e else (lane < HEAD_DIM), q, jnp.zeros_like(q)) for e in range(PAIR)]
    return qs, lambda c, e: k_ref[_key_rows(c), :]


def _wide_pair(q_ref, k_ref):
    qs = [q_ref[:, e * LANES:(e + 1) * LANES] for e in range(PAIR)]
    return qs, lambda c, e: k_ref[_key_rows(c), e * LANES:(e + 1) * LANES]


def _value_t(vt_ref, c, e):
    return vt_ref[c, e * HEAD_DIM:(e + 1) * HEAD_DIM, :]


def _softmax_step(carry, scores, vts):
    m, l, acc = carry
    m_new = m
    for s in scores:
        m_new = jnp.maximum(m_new, jnp.max(s, axis=0, keepdims=True))
    alpha = jnp.exp(m - m_new)
    l = alpha * l
    acc = alpha * acc
    for s, vt in zip(scores, vts):
        p = jnp.exp(s - m_new)
        l = l + jnp.sum(p, axis=0, keepdims=True)
        acc = acc + _dot(vt, p.astype(BF16))
    return m_new, l, acc


def _softmax_init():
    t = ATT_TILE
    one = (jnp.full((1, t), MASKED, F32), jnp.zeros((1, t), F32), jnp.zeros((HEAD_DIM, t), F32))
    return (one,) * PAIR


def _diag_positions(c):
    key = lax.broadcasted_iota(jnp.int32, (SEQ_TILE, ATT_TILE), 0) + c * SEQ_TILE
    qry = lax.broadcasted_iota(jnp.int32, (SEQ_TILE, ATT_TILE), 1)
    return key, qry


def _softmax_attention(q_pair, vt_ref, o_ref, diag_bias, past_bias):
    qs, keys = q_pair
    i = pl.program_id(1)

    def slab(j, carry, bias):
        out = []
        for e in range(PAIR):
            cs = [j * CHUNKS + c for c in range(CHUNKS)]
            scores = [bias(e, c, cg, _dot_nt(keys(cg, e), qs[e])) for c, cg in enumerate(cs)]
            out.append(_softmax_step(carry[e], scores, [_value_t(vt_ref, cg, e) for cg in cs]))
        return tuple(out)

    carry = slab(i, _softmax_init(), diag_bias)
    carry = lax.fori_loop(0, i, lambda j, carry: slab(j, carry, past_bias), carry)
    for e in range(PAIR):
        _, l, acc = carry[e]
        o_ref[e * HEAD_DIM:(e + 1) * HEAD_DIM, :] = (acc / l).astype(o_ref.dtype)


def _moba_kernel(q_ref, k_ref, vt_ref, sel_ref, o_ref):
    def selected(e, cg):
        return sel_ref[e, pl.ds(cg, 1), :]

    def diag_bias(e, c, cg, s):
        key, qry = _diag_positions(c)
        own = lax.shift_right_logical(qry, MOBA_BLOCK.bit_length() - 1) == c
        return jnp.where(own, jnp.where(key <= qry, s, MASKED), s + selected(e, cg))

    def past_bias(e, c, cg, s):
        return s + selected(e, cg)

    _softmax_attention(_packed_pair(q_ref, k_ref), vt_ref, o_ref, diag_bias, past_bias)


def _mla_kernel(q_ref, k_ref, vt_ref, o_ref):
    def diag_bias(e, c, cg, s):
        key, qry = _diag_positions(c)
        return jnp.where(key <= qry, s, MASKED)

    _softmax_attention(_wide_pair(q_ref, k_ref), vt_ref, o_ref, diag_bias, lambda e, c, cg, s: s)


def _log1m_beta(z):
    return -(jnp.maximum(z, 0.0) + jnp.log1p(jnp.exp(-jnp.abs(z))))


def _sb_chunk(carry, z, keep, vt, upper):
    c, acc = carry
    log1m = _log1m_beta(z)
    if keep is not None:
        log1m = jnp.where(keep, log1m, 0.0)
    hi = log1m.astype(BF16)
    lo = (log1m - hi.astype(F32)).astype(BF16)
    suffix = _dot(upper, hi) + _dot(upper, lo) + c
    wgt = jnp.exp(z + log1m + suffix)
    if keep is not None:
        wgt = jnp.where(keep, wgt, 0.0)
    acc = acc + _dot(vt, wgt.astype(BF16))
    c = c + jnp.sum(log1m, axis=0, keepdims=True)
    return c, acc


def _sb_kernel(q_ref, k_ref, vt_ref, up_ref, o_ref):
    i = pl.program_id(1)
    qs, keys = _packed_pair(q_ref, k_ref)
    upper = up_ref[...]

    def slab(j, carry, diagonal):
        out = []
        for e in range(PAIR):
            ce = carry[e]
            for c in reversed(range(CHUNKS)):
                cg = j * CHUNKS + c
                keep = None
                if diagonal:
                    key, qry = _diag_positions(c)
                    keep = key < qry
                ce = _sb_chunk(ce, _dot_nt(keys(cg, e), qs[e]), keep, _value_t(vt_ref, cg, e), upper)
            out.append(ce)
        return tuple(out)

    init = ((jnp.zeros((1, ATT_TILE), F32), jnp.zeros((HEAD_DIM, ATT_TILE), F32)),) * PAIR
    carry = slab(i, init, True)
    carry = lax.fori_loop(0, i, lambda n, carry: slab(i - 1 - n, carry, False), carry)
    for e in range(PAIR):
        o_ref[e * HEAD_DIM:(e + 1) * HEAD_DIM, :] = carry[e][1].astype(o_ref.dtype)


def _attention(kernel, q, k, vt, extra, extra_specs, pair_lanes, name):
    s = q.shape[0]
    nb = s // SEQ_TILE
    rows = PAIR * HEAD_DIM
    return pl.pallas_call(
        kernel,
        grid=(N_HEADS // PAIR, s // ATT_TILE),
        in_specs=[
            pl.BlockSpec((ATT_TILE, pair_lanes), lambda p, i: (i, p)),
            pl.BlockSpec((s, pair_lanes), lambda p, i: (0, p)),
            pl.BlockSpec((nb, rows, SEQ_TILE), lambda p, i: (0, p, 0)),
        ] + extra_specs,
        out_specs=pl.BlockSpec((rows, ATT_TILE), lambda p, i: (p, i)),
        out_shape=jax.ShapeDtypeStruct((N_HEADS * HEAD_DIM, s), BF16),
        compiler_params=_params(2),
        name=name,
    )(q, k, vt, *extra)


def _merge_kernel(x_ref, ya_ref, yb_ref, yc_ref, gt_ref, wbt_ref, wot_ref, mod_ref, gpost_ref, o_ref):
    d = D_MODEL
    merged_t = None
    for n, y_ref in enumerate((ya_ref, yb_ref, yc_ref)):
        part = _dot(wbt_ref[n], y_ref[...]) * gt_ref[n * d:(n + 1) * d, :].astype(F32)
        merged_t = part if merged_t is None else merged_t + part
    out = _dot(wot_ref[...], merged_t.astype(BF16)).T
    gate = mod_ref[2:3, :]
    o_ref[...] = x_ref[...] + (1.0 + gate) * _rms_norm(out, gpost_ref[...])


def _merge(x, ya, yb, yc, gt, wbt, wot, mod, g_post):
    s, d = x.shape
    t = SEQ_TILE
    y_spec = pl.BlockSpec((BRANCH_W, t), lambda i: (0, i))
    return pl.pallas_call(
        _merge_kernel,
        grid=(s // t,),
        in_specs=[
            pl.BlockSpec((t, d), lambda i: (i, 0)),
            y_spec, y_spec, y_spec,
            pl.BlockSpec((N_BRANCH * d, t), lambda i: (0, i)),
            _const_spec(wbt.shape), _const_spec(wot.shape), _const_spec((3, d)), _const_spec((1, d)),
        ],
        out_specs=pl.BlockSpec((t, d), lambda i: (i, 0)),
        out_shape=jax.ShapeDtypeStruct((s, d), F32),
        compiler_params=_params(1),
        name="mixer_merge",
    )(x, ya, yb, yc, gt, wbt, wot, mod, g_post.reshape(1, d))


def _rope_tables(n_rot, seq, first_lane, period):
    half = n_rot // 2
    inv = jnp.power(jnp.float32(ROPE_THETA), -jnp.arange(0, n_rot, 2, dtype=F32) / n_rot)
    ang = jnp.arange(seq, dtype=F32)[:, None] * inv[None, :]
    cos, sin = jnp.cos(ang), jnp.sin(ang)
    lane = jnp.arange(LANES) % period - first_lane
    lo = (lane >= 0) & (lane < half)
    hi = (lane >= half) & (lane < n_rot)
    idx = jnp.clip(lane, 0, n_rot - 1) % half
    a = jnp.where((lo | hi)[None, :], cos[:, idx], 1.0)
    b = jnp.where(lo[None, :], -sin[:, idx], 0.0)
    c = jnp.where(hi[None, :], sin[:, idx], 0.0)
    return a, b, c


def _pad_heads(w, n_heads, width):
    k = w.shape[0]
    w = w.reshape(k, n_heads, width)
    return jnp.pad(w, ((0, 0), (0, 0), (0, HEAD_PAD - width))).reshape(k, n_heads * HEAD_PAD)


def _mixer_weights(w_in, b_gate, q_norm, w_uq, kv_norm, w_ukv, w_branch, w_out):
    w = BRANCH_W
    d = D_MODEL
    qa, ka, va, qb, kb, vb = (w_in[:, n * w:(n + 1) * w] for n in range(6))
    o = 6 * w
    w_cq = w_in[:, o:o + MLA_Q_LORA]
    o += MLA_Q_LORA
    w_ckv = w_in[:, o:o + MLA_KV_LORA]
    o += MLA_KV_LORA
    w_kr = jnp.pad(w_in[:, o:o + MLA_ROPE], ((0, 0), (MLA_NOPE, HEAD_PAD - MLA_QK)))
    o += MLA_ROPE
    w_gates = w_in[:, o:]
    ukv = w_ukv.reshape(MLA_KV_LORA, N_HEADS, MLA_NOPE + MLA_V)
    return {
        "w_row": jnp.concatenate([qa, ka, qb, kb, w_cq, w_ckv, w_kr], axis=1).astype(BF16),
        "w_t": jnp.concatenate([va, vb, w_gates], axis=1).T.astype(BF16),
        "w_uq": _pad_heads(w_uq, N_HEADS, MLA_QK).astype(BF16),
        "w_uk": _pad_heads(ukv[:, :, :MLA_NOPE].reshape(MLA_KV_LORA, -1), N_HEADS, MLA_NOPE).astype(BF16),
        "w_uvt": ukv[:, :, MLA_NOPE:].reshape(MLA_KV_LORA, -1).T.astype(BF16),
        "q_norm": q_norm.reshape(1, -1),
        "kv_norm": kv_norm.reshape(1, -1),
        "b_gate": b_gate.reshape(N_BRANCH * d, 1),
        "w_bt": jnp.swapaxes(w_branch, 1, 2).astype(BF16),
        "w_ot": w_out.T.astype(BF16),
    }


def kernel(x, c, ada_w, ada_b, norm_pre, norm_post, ffn_w_gate, ffn_w_up, ffn_w_down, mix_w_in, mix_b_gate,
           mla_q_norm, mla_w_uq, mla_kv_norm, mla_w_ukv, mix_w_branch, mix_w_out):
    bsz, s, d = x.shape
    assert bsz == 1 and d == D_MODEL and s % ATT_TILE == 0 and s % min(FFN_TILE, s) == 0
    depth = ada_w.shape[0]
    t = SEQ_TILE
    mod = _modulation(c, ada_w, ada_b)
    tabs = _rope_tables(PARTIAL_ROT, s, 0, HEAD_DIM) + _rope_tables(MLA_ROPE, s, MLA_NOPE, HEAD_PAD)
    upper = (jnp.arange(t)[None, :] > jnp.arange(t)[:, None]).astype(BF16)
    xs = x.reshape(s, d)
    for l in range(depth):
        xs = _ffn(xs, mod[l, 0], norm_pre[l, 0], norm_post[l, 0], ffn_w_gate[l, 0].astype(BF16),
                  ffn_w_up[l, 0].astype(BF16), ffn_w_down[l, 0].astype(BF16), 0.5)
        wp = _mixer_weights(mix_w_in[l], mix_b_gate[l], mla_q_norm[l], mla_w_uq[l], mla_kv_norm[l], mla_w_ukv[l],
                            mix_w_branch[l], mix_w_out[l])
        qa, ka, vat, sel, qb, kb, vbt, qc, kc, vct, gt = _mixer_proj(xs, mod[l, 1], norm_pre[l, 1], wp, tabs)
        ya = _attention(_moba_kernel, qa, ka, vat, (sel,),
                        [pl.BlockSpec((PAIR, s // t, ATT_TILE), lambda p, i: (p, 0, i))], LANES, "moba_attention")
        yb = _attention(_sb_kernel, qb, kb, vbt, (upper,), [_const_spec((t, t))], LANES, "stick_breaking_attention")
        yc = _attention(_mla_kernel, qc, kc, vct, (), [], PAIR * HEAD_PAD, "mla_attention")
        xs = _merge(xs, ya, yb, yc, gt, wp["w_bt"], wp["w_ot"], mod[l, 1], norm_post[l, 1])
        xs = _ffn(xs, mod[l, 2], norm_pre[l, 2], norm_post[l, 2], ffn_w_gate[l, 1].astype(BF16),
                  ffn_w_up[l, 1].astype(BF16), ffn_w_down[l, 1].astype(BF16), 0.5)
    return xs.reshape(bsz, s, d)
```

```python
import functools

import jax
import jax.numpy as jnp
from jax import lax
from jax.experimental import pallas as pl
from jax.experimental.pallas import tpu as pltpu

F32 = jnp.float32
BF16 = jnp.bfloat16

D_MODEL = 1024
N_HEADS = 8
HEAD_DIM = 64
MOBA_BLOCK = 256
MOBA_TOPK = 3
MLA_Q_LORA = 256
MLA_KV_LORA = 128
MLA_NOPE = 64
MLA_ROPE = 32
MLA_V = 64
MLA_QK = MLA_NOPE + MLA_ROPE
ROPE_THETA = 500000.0
PARTIAL_ROT = HEAD_DIM // 4
D_FF = 2816
N_BRANCH = 3
BRANCH_W = N_HEADS * HEAD_DIM
N_SUB = 3
EPS = 1e-6

LANES = 128
V7X_VMEM_LIMIT = 56 * 1024 * 1024

SEQ_TILE = MOBA_BLOCK
FFN_TILE = 512
MASKED = -1e30
HEAD_PAD = LANES


def _dot(a, b):
    return jnp.dot(a, b, preferred_element_type=F32)


def _dot_nt(a, b, precision=None):
    return lax.dot_general(a, b, (((1,), (1,)), ((), ())), precision=precision, preferred_element_type=F32)


def _rms_norm(x, g):
    return x * lax.rsqrt(jnp.mean(x * x, axis=-1, keepdims=True) + EPS) * g


def _const_spec(shape):
    nd = len(shape)
    return pl.BlockSpec(shape, lambda *_: (0,) * nd, pipeline_mode=pl.Buffered(1))


def _params(n_grid):
    return pltpu.CompilerParams(dimension_semantics=("arbitrary",) * n_grid, vmem_limit_bytes=V7X_VMEM_LIMIT)


def _mod_kernel(c_ref, w_ref, b_ref, o_ref):
    c = c_ref[...]
    ca = c * jax.nn.sigmoid(c)
    o_ref[0] = jnp.dot(ca, w_ref[0], precision=lax.Precision.HIGHEST, preferred_element_type=F32) + b_ref[0]


def _modulation(c, ada_w, ada_b):
    depth, d, n = ada_w.shape
    tn = n // 8
    c8 = jnp.broadcast_to(c.reshape(1, d), (8, d))
    out = pl.pallas_call(
        _mod_kernel,
        grid=(depth, n // tn),
        in_specs=[
            pl.BlockSpec((8, d), lambda l, j: (0, 0)),
            pl.BlockSpec((1, d, tn), lambda l, j: (l, 0, j)),
            pl.BlockSpec((1, 1, tn), lambda l, j: (l, 0, j)),
        ],
        out_specs=pl.BlockSpec((1, 8, tn), lambda l, j: (l, 0, j)),
        out_shape=jax.ShapeDtypeStruct((depth, 8, n), F32),
        compiler_params=_params(2),
        name="adaln_mod",
    )(c8, ada_w, ada_b.reshape(depth, 1, n))
    return out[:, 0, :].reshape(depth, N_SUB, 3, d)


def _ffn_kernel(x_ref, mod_ref, gpre_ref, gpost_ref, wg_ref, wu_ref, wd_ref, o_ref, *, res_w):
    x = x_ref[...]
    shift, scale, gate = mod_ref[0:1, :], mod_ref[1:2, :], mod_ref[2:3, :]
    h = (_rms_norm(x, gpre_ref[...]) * (1.0 + scale) + shift).astype(BF16)
    g = _dot(h, wg_ref[...])
    u = _dot(h, wu_ref[...])
    a = (g * jax.nn.sigmoid(g) * u).astype(BF16)
    y = _dot(a, wd_ref[...])
    o_ref[...] = x + (res_w * (1.0 + gate)) * _rms_norm(y, gpost_ref[...])


def _ffn(x, mod, g_pre, g_post, wg, wu, wd, res_w):
    s, d = x.shape
    tm = min(FFN_TILE, s)
    dff = wg.shape[1]
    return pl.pallas_call(
        functools.partial(_ffn_kernel, res_w=res_w),
        grid=(s // tm,),
        in_specs=[
            pl.BlockSpec((tm, d), lambda i: (i, 0)),
            _const_spec((3, d)),
            _const_spec((1, d)),
            _const_spec((1, d)),
            _const_spec((d, dff)),
            _const_spec((d, dff)),
            _const_spec((dff, d)),
        ],
        out_specs=pl.BlockSpec((tm, d), lambda i: (i, 0)),
        out_shape=jax.ShapeDtypeStruct((s, d), F32),
        compiler_params=_params(1),
        name="macaron_ffn",
    )(x, mod, g_pre.reshape(1, d), g_post.reshape(1, d), wg, wu, wd)


def _rope_rows(x, a, b, c, half):
    outs = []
    for g in range(x.shape[1] // LANES):
        xg = x[:, g * LANES:(g + 1) * LANES]
        outs.append(xg * a + pltpu.roll(xg, LANES - half, 1) * b + pltpu.roll(xg, half, 1) * c)
    return jnp.concatenate(outs, axis=1) if len(outs) > 1 else outs[0]


def _proj_kernel(x_ref, mod_ref, gpre_ref, wrow_ref, wt_ref, wuq_ref, wuk_ref, wuvt_ref, qn_ref, kvn_ref, bg_ref,
                 pa_ref, pb_ref, pc_ref, ma_ref, mb_ref, mc_ref,
                 qa_ref, ka_ref, vat_ref, sel_ref, qb_ref, kb_ref, vbt_ref, qc_ref, kc_ref, vct_ref, gt_ref,
                 kmean_ref, *, n_blocks):
    i = pl.program_id(0)
    w = BRANCH_W
    x = x_ref[...]
    shift, scale = mod_ref[0:1, :], mod_ref[1:2, :]
    hn = (_rms_norm(x, gpre_ref[...]) * (1.0 + scale) + shift).astype(BF16)
    rows = _dot(hn, wrow_ref[...])
    cols_t = _dot_nt(wt_ref[...], hn)

    pa, pb, pc = pa_ref[...], pb_ref[...], pc_ref[...]
    q_a = _rope_rows(rows[:, 0:w], pa, pb, pc, PARTIAL_ROT // 2)
    k_a = _rope_rows(rows[:, w:2 * w], pa, pb, pc, PARTIAL_ROT // 2)
    qa_ref[...] = (q_a * HEAD_DIM ** -0.5).astype(BF16)
    ka_ref[...] = k_a.astype(BF16)
    vat_ref[0] = cols_t[0:w, :].astype(BF16)

    @pl.when(i == 0)
    def _():
        kmean_ref[...] = jnp.zeros_like(kmean_ref)

    kmean_ref[pl.ds(i, 1), :] = jnp.mean(k_a, axis=0, keepdims=True)
    kmean = kmean_ref[...]
    t = x.shape[0]
    lane = lax.broadcasted_iota(jnp.int32, (1, LANES), 1)
    blk = lax.broadcasted_iota(jnp.int32, (n_blocks, t), 0)
    past = blk < i
    blk_f = blk.astype(F32)
    for h in range(N_HEADS):
        p, e = h // 2, h % 2
        in_head = (lane >= HEAD_DIM) if e else (lane < HEAD_DIM)
        qm = jnp.where(in_head, q_a[:, p * LANES:(p + 1) * LANES], 0.0)
        gate = _dot_nt(kmean[:, p * LANES:(p + 1) * LANES], qm, precision=lax.Precision.HIGHEST)
        gate = jnp.where(past, gate, -jnp.inf)
        chosen = jnp.zeros(gate.shape, F32)
        for _ in range(min(MOBA_TOPK, n_blocks)):
            top = jnp.max(gate, axis=0, keepdims=True)
            first = jnp.min(jnp.where(gate == top, blk_f, float(n_blocks)), axis=0, keepdims=True)
            pick = blk_f == first
            chosen = jnp.where(pick, 1.0, chosen)
            gate = jnp.where(pick, -jnp.inf, gate)
        sel_ref[h] = jnp.where(jnp.logical_and(chosen > 0.0, past), 0.0, MASKED)

    qb_ref[...] = (rows[:, 2 * w:3 * w] * HEAD_DIM ** -0.5).astype(BF16)
    kb_ref[...] = rows[:, 3 * w:4 * w].astype(BF16)
    vbt_ref[0] = cols_t[w:2 * w, :].astype(BF16)

    o = 4 * w
    ma, mb, mc = ma_ref[...], mb_ref[...], mc_ref[...]
    cq = _rms_norm(rows[:, o:o + MLA_Q_LORA], qn_ref[...]).astype(BF16)
    q_c = _rope_rows(_dot(cq, wuq_ref[...]), ma, mb, mc, MLA_ROPE // 2)
    qc_ref[...] = (q_c * MLA_QK ** -0.5).astype(BF16)
    o += MLA_Q_LORA
    ckv = _rms_norm(rows[:, o:o + MLA_KV_LORA], kvn_ref[...]).astype(BF16)
    o += MLA_KV_LORA
    k_rope = _rope_rows(rows[:, o:o + HEAD_PAD], ma, mb, mc, MLA_ROPE // 2)
    kc_ref[...] = (_dot(ckv, wuk_ref[...]) + jnp.concatenate([k_rope] * N_HEADS, axis=1)).astype(BF16)
    vct_ref[0] = _dot_nt(wuvt_ref[...], ckv).astype(BF16)

    gt_ref[...] = jax.nn.sigmoid(cols_t[2 * w:, :] + bg_ref[...]).astype(BF16)


def _mixer_proj(x, mod, g_pre, wp, tabs):
    s, d = x.shape
    t = SEQ_TILE
    nb = s // t
    w = BRANCH_W
    hp = N_HEADS * HEAD_PAD
    row_spec = lambda n: pl.BlockSpec((t, n), lambda i: (i, 0))
    vt_spec = pl.BlockSpec((1, w, t), lambda i: (i, 0, 0))
    tab_spec = pl.BlockSpec((t, LANES), lambda i: (i, 0))
    out_shape = (
        jax.ShapeDtypeStruct((s, w), BF16), jax.ShapeDtypeStruct((s, w), BF16), jax.ShapeDtypeStruct((nb, w, t), BF16),
        jax.ShapeDtypeStruct((N_HEADS, nb, s), F32),
        jax.ShapeDtypeStruct((s, w), BF16), jax.ShapeDtypeStruct((s, w), BF16), jax.ShapeDtypeStruct((nb, w, t), BF16),
        jax.ShapeDtypeStruct((s, hp), BF16), jax.ShapeDtypeStruct((s, hp), BF16), jax.ShapeDtypeStruct((nb, w, t), BF16),
        jax.ShapeDtypeStruct((N_BRANCH * d, s), BF16),
    )
    out_specs = (
        row_spec(w), row_spec(w), vt_spec,
        pl.BlockSpec((N_HEADS, nb, t), lambda i: (0, 0, i)),
        row_spec(w), row_spec(w), vt_spec,
        row_spec(hp), row_spec(hp), vt_spec,
        pl.BlockSpec((N_BRANCH * d, t), lambda i: (0, i)),
    )
    in_specs = [
        pl.BlockSpec((t, d), lambda i: (i, 0)),
        _const_spec((3, d)), _const_spec((1, d)),
        _const_spec(wp["w_row"].shape), _const_spec(wp["w_t"].shape), _const_spec(wp["w_uq"].shape),
        _const_spec(wp["w_uk"].shape), _const_spec(wp["w_uvt"].shape),
        _const_spec((1, MLA_Q_LORA)), _const_spec((1, MLA_KV_LORA)), _const_spec((N_BRANCH * d, 1)),
    ] + [tab_spec] * 6
    return pl.pallas_call(
        functools.partial(_proj_kernel, n_blocks=nb),
        grid=(nb,),
        in_specs=in_specs,
        out_specs=out_specs,
        out_shape=out_shape,
        scratch_shapes=[pltpu.VMEM((nb, w), F32)],
        compiler_params=_params(1),
        name="mixer_proj",
    )(x, mod, g_pre.reshape(1, d), wp["w_row"], wp["w_t"], wp["w_uq"], wp["w_uk"], wp["w_uvt"],
      wp["q_norm"], wp["kv_norm"], wp["b_gate"], *tabs)


ATT_TILE = 512
CHUNKS = ATT_TILE // SEQ_TILE
PAIR = 2
SB_EXP_IS_ZERO = -104.0


def _key_rows(c):
    return pl.ds(pl.multiple_of(c * SEQ_TILE, SEQ_TILE), SEQ_TILE)


def _packed_pair(q_ref, k_ref):
    q = q_ref[...]
    lane = lax.broadcasted_iota(jnp.int32, (1, LANES), 1)
    qs = [jnp.where((lane >= HEAD_DIM) if e else (lane < HEAD_DIM), q, jnp.zeros_like(q)) for e in range(PAIR)]
    return qs, lambda c, e: k_ref[_key_rows(c), :]


def _wide_pair(q_ref, k_ref):
    qs = [q_ref[:, e * LANES:(e + 1) * LANES] for e in range(PAIR)]
    return qs, lambda c, e: k_ref[_key_rows(c), e * LANES:(e + 1) * LANES]


def _value_t(vt_ref, c, e):
    return vt_ref[c, e * HEAD_DIM:(e + 1) * HEAD_DIM, :]


def _diag_positions(c):
    key = lax.broadcasted_iota(jnp.int32, (SEQ_TILE, ATT_TILE), 0) + c * SEQ_TILE
    qry = lax.broadcasted_iota(jnp.int32, (SEQ_TILE, ATT_TILE), 1)
    return key, qry


def _softmax_attention(q_pair, vt_ref, o_ref, s_refs, diag_bias, past_bias):
    qs, keys = q_pair
    i = pl.program_id(1)

    def score(e, j, bias):
        col_max = None
        for c in range(CHUNKS):
            cg = j * CHUNKS + c
            s = bias(e, c, cg, _dot_nt(keys(cg, e), qs[e]))
            s_refs[e][c * SEQ_TILE:(c + 1) * SEQ_TILE, :] = s
            cm = jnp.max(s, axis=0, keepdims=True)
            col_max = cm if col_max is None else jnp.maximum(col_max, cm)
        return col_max

    def update(e, j, carry, col_max):
        m, l, acc = carry
        m_new = jnp.maximum(m, col_max)
        alpha = jnp.exp(m - m_new)
        l = alpha * l
        acc = alpha * acc
        for c in range(CHUNKS):
            p = jnp.exp(s_refs[e][c * SEQ_TILE:(c + 1) * SEQ_TILE, :] - m_new)
            l = l + jnp.sum(p, axis=0, keepdims=True)
            acc = acc + _dot(_value_t(vt_ref, j * CHUNKS + c, e), p.astype(BF16))
        return m_new, l, acc

    t = ATT_TILE
    init = (jnp.full((1, t), MASKED, F32), jnp.zeros((1, t), F32), jnp.zeros((HEAD_DIM, t), F32))
    last = jnp.maximum(i - 1, 0)
    max0 = score(0, i, diag_bias)
    max1 = score(1, i, diag_bias)
    c0 = update(0, i, init, max0)
    max0 = score(0, 0, past_bias)
    c1 = update(1, i, init, max1)

    def past_slab(j, state):
        c0, c1, max0 = state
        max1 = score(1, j, past_bias)
        c0 = update(0, j, c0, max0)
        max0 = score(0, jnp.minimum(j + 1, last), past_bias)
        c1 = update(1, j, c1, max1)
        return c0, c1, max0

    c0, c1, _ = lax.fori_loop(0, i, past_slab, (c0, c1, max0))
    for e, (_, l, acc) in enumerate((c0, c1)):
        o_ref[e * HEAD_DIM:(e + 1) * HEAD_DIM, :] = (acc / l).astype(o_ref.dtype)


def _moba_kernel(q_ref, k_ref, vt_ref, sel_ref, o_ref, s0_ref, s1_ref):
    def selected(e, cg):
        return sel_ref[e, pl.ds(cg, 1), :]

    def diag_bias(e, c, cg, s):
        key, qry = _diag_positions(c)
        own = lax.shift_right_logical(qry, MOBA_BLOCK.bit_length() - 1) == c
        return jnp.where(own, jnp.where(key <= qry, s, MASKED), s + selected(e, cg))

    def past_bias(e, c, cg, s):
        return s + selected(e, cg)

    _softmax_attention(_packed_pair(q_ref, k_ref), vt_ref, o_ref, (s0_ref, s1_ref), diag_bias, past_bias)


def _mla_kernel(q_ref, k_ref, vt_ref, o_ref, s0_ref, s1_ref):
    def diag_bias(e, c, cg, s):
        key, qry = _diag_positions(c)
        return jnp.where(key <= qry, s, MASKED)

    _softmax_attention(_wide_pair(q_ref, k_ref), vt_ref, o_ref, (s0_ref, s1_ref), diag_bias, lambda e, c, cg, s: s)


def _log1m_beta(z):
    return -(jnp.maximum(z, 0.0) + jnp.log1p(jnp.exp(-jnp.abs(z))))


def _sb_chunk(carry, z, keep, vt, upper):
    c, acc = carry
    log1m = _log1m_beta(z)
    if keep is not None:
        log1m = jnp.where(keep, log1m, 0.0)
    hi = log1m.astype(BF16)
    lo = (log1m - hi.astype(F32)).astype(BF16)
    suffix = _dot(upper, hi) + _dot(upper, lo) + c
    wgt = jnp.exp(z + log1m + suffix)
    if keep is not None:
        wgt = jnp.where(keep, wgt, 0.0)
    acc = acc + _dot(vt, wgt.astype(BF16))
    c = c + jnp.sum(log1m, axis=0, keepdims=True)
    return c, acc


def _sb_kernel(q_ref, k_ref, vt_ref, up_ref, o_ref):
    i = pl.program_id(1)
    qs, keys = _packed_pair(q_ref, k_ref)
    upper = up_ref[...]

    def slab(j, carry, diagonal):
        out = []
        for e in range(PAIR):
            ce = carry[e]
            for c in reversed(range(CHUNKS)):
                cg = j * CHUNKS + c
                keep = None
                if diagonal:
                    key, qry = _diag_positions(c)
                    keep = key < qry
                ce = _sb_chunk(ce, _dot_nt(keys(cg, e), qs[e]), keep, _value_t(vt_ref, cg, e), upper)
            out.append(ce)
        return tuple(out)

    init = ((jnp.zeros((1, ATT_TILE), F32), jnp.zeros((HEAD_DIM, ATT_TILE), F32)),) * PAIR
    carry = slab(i, init, True)

    def alive(state):
        n, carry = state
        c_max = jnp.max(jnp.maximum(carry[0][0], carry[1][0]))
        return jnp.logical_and(n < i, c_max > SB_EXP_IS_ZERO)

    _, carry = lax.while_loop(alive, lambda st: (st[0] + 1, slab(i - 1 - st[0], st[1], False)), (0, carry))
    for e in range(PAIR):
        o_ref[e * HEAD_DIM:(e + 1) * HEAD_DIM, :] = carry[e][1].astype(o_ref.dtype)


def _attention(kernel, q, k, vt, extra, extra_specs, pair_lanes, name, score_scratch=True):
    scratch = [pltpu.VMEM((ATT_TILE, ATT_TILE), F32)] * PAIR if score_scratch else []
    s = q.shape[0]
    nb = s // SEQ_TILE
    rows = PAIR * HEAD_DIM
    return pl.pallas_call(
        kernel,
        grid=(N_HEADS // PAIR, s // ATT_TILE),
        in_specs=[
            pl.BlockSpec((ATT_TILE, pair_lanes), lambda p, i: (i, p)),
            pl.BlockSpec((s, pair_lanes), lambda p, i: (0, p)),
            pl.BlockSpec((nb, rows, SEQ_TILE), lambda p, i: (0, p, 0)),
        ] + extra_specs,
        out_specs=pl.BlockSpec((rows, ATT_TILE), lambda p, i: (p, i)),
        out_shape=jax.ShapeDtypeStruct((N_HEADS * HEAD_DIM, s), BF16),
        scratch_shapes=scratch,
        compiler_params=_params(2),
        name=name,
    )(q, k, vt, *extra)


def _merge_kernel(x_ref, ya_ref, yb_ref, yc_ref, gt_ref, wbt_ref, wot_ref, mod_ref, gpost_ref, o_ref):
    d = D_MODEL
    merged_t = None
    for n, y_ref in enumerate((ya_ref, yb_ref, yc_ref)):
        part = _dot(wbt_ref[n], y_ref[...]) * gt_ref[n * d:(n + 1) * d, :].astype(F32)
        merged_t = part if merged_t is None else merged_t + part
    out = _dot(wot_ref[...], merged_t.astype(BF16)).T
    gate = mod_ref[2:3, :]
    o_ref[...] = x_ref[...] + (1.0 + gate) * _rms_norm(out, gpost_ref[...])


def _merge(x, ya, yb, yc, gt, wbt, wot, mod, g_post):
    s, d = x.shape
    t = SEQ_TILE
    y_spec = pl.BlockSpec((BRANCH_W, t), lambda i: (0, i))
    return pl.pallas_call(
        _merge_kernel,
        grid=(s // t,),
        in_specs=[
            pl.BlockSpec((t, d), lambda i: (i, 0)),
            y_spec, y_spec, y_spec,
            pl.BlockSpec((N_BRANCH * d, t), lambda i: (0, i)),
            _const_spec(wbt.shape), _const_spec(wot.shape), _const_spec((3, d)), _const_spec((1, d)),
        ],
        out_specs=pl.BlockSpec((t, d), lambda i: (i, 0)),
        out_shape=jax.ShapeDtypeStruct((s, d), F32),
        compiler_params=_params(1),
        name="mixer_merge",
    )(x, ya, yb, yc, gt, wbt, wot, mod, g_post.reshape(1, d))


def _rope_tables(n_rot, seq, first_lane, period):
    half = n_rot // 2
    inv = jnp.power(jnp.float32(ROPE_THETA), -jnp.arange(0, n_rot, 2, dtype=F32) / n_rot)
    ang = jnp.arange(seq, dtype=F32)[:, None] * inv[None, :]
    cos, sin = jnp.cos(ang), jnp.sin(ang)
    lane = jnp.arange(LANES) % period - first_lane
    lo = (lane >= 0) & (lane < half)
    hi = (lane >= half) & (lane < n_rot)
    idx = jnp.clip(lane, 0, n_rot - 1) % half
    a = jnp.where((lo | hi)[None, :], cos[:, idx], 1.0)
    b = jnp.where(lo[None, :], -sin[:, idx], 0.0)
    c = jnp.where(hi[None, :], sin[:, idx], 0.0)
    return a, b, c


def _pad_heads(w, n_heads, width):
    k = w.shape[0]
    w = w.reshape(k, n_heads, width)
    return jnp.pad(w, ((0, 0), (0, 0), (0, HEAD_PAD - width))).reshape(k, n_heads * HEAD_PAD)


def _mixer_weights(w_in, b_gate, q_norm, w_uq, kv_norm, w_ukv, w_branch, w_out):
    w = BRANCH_W
    d = D_MODEL
    qa, ka, va, qb, kb, vb = (w_in[:, n * w:(n + 1) * w] for n in range(6))
    o = 6 * w
    w_cq = w_in[:, o:o + MLA_Q_LORA]
    o += MLA_Q_LORA
    w_ckv = w_in[:, o:o + MLA_KV_LORA]
    o += MLA_KV_LORA
    w_kr = jnp.pad(w_in[:, o:o + MLA_ROPE], ((0, 0), (MLA_NOPE, HEAD_PAD - MLA_QK)))
    o += MLA_ROPE
    w_gates = w_in[:, o:]
    ukv = w_ukv.reshape(MLA_KV_LORA, N_HEADS, MLA_NOPE + MLA_V)
    return {
        "w_row": jnp.concatenate([qa, ka, qb, kb, w_cq, w_ckv, w_kr], axis=1).astype(BF16),
        "w_t": jnp.concatenate([va, vb, w_gates], axis=1).T.astype(BF16),
        "w_uq": _pad_heads(w_uq, N_HEADS, MLA_QK).astype(BF16),
        "w_uk": _pad_heads(ukv[:, :, :MLA_NOPE].reshape(MLA_KV_LORA, -1), N_HEADS, MLA_NOPE).astype(BF16),
        "w_uvt": ukv[:, :, MLA_NOPE:].reshape(MLA_KV_LORA, -1).T.astype(BF16),
        "q_norm": q_norm.reshape(1, -1),
        "kv_norm": kv_norm.reshape(1, -1),
        "b_gate": b_gate.reshape(N_BRANCH * d, 1),
        "w_bt": jnp.swapaxes(w_branch, 1, 2).astype(BF16),
        "w_ot": w_out.T.astype(BF16),
    }


def kernel(x, c, ada_w, ada_b, norm_pre, norm_post, ffn_w_gate, ffn_w_up, ffn_w_down, mix_w_in, mix_b_gate,
           mla_q_norm, mla_w_uq, mla_kv_norm, mla_w_ukv, mix_w_branch, mix_w_out):
    bsz, s, d = x.shape
    assert bsz == 1 and d == D_MODEL and s % ATT_TILE == 0 and s % min(FFN_TILE, s) == 0
    depth = ada_w.shape[0]
    t = SEQ_TILE
    mod = _modulation(c, ada_w, ada_b)
    tabs = _rope_tables(PARTIAL_ROT, s, 0, HEAD_DIM) + _rope_tables(MLA_ROPE, s, MLA_NOPE, HEAD_PAD)
    upper = (jnp.arange(t)[None, :] > jnp.arange(t)[:, None]).astype(BF16)
    xs = x.reshape(s, d)
    for l in range(depth):
        xs = _ffn(xs, mod[l, 0], norm_pre[l, 0], norm_post[l, 0], ffn_w_gate[l, 0].astype(BF16),
                  ffn_w_up[l, 0].astype(BF16), ffn_w_down[l, 0].astype(BF16), 0.5)
        wp = _mixer_weights(mix_w_in[l], mix_b_gate[l], mla_q_norm[l], mla_w_uq[l], mla_kv_norm[l], mla_w_ukv[l],
                            mix_w_branch[l], mix_w_out[l])
        qa, ka, vat, sel, qb, kb, vbt, qc, kc, vct, gt = _mixer_proj(xs, mod[l, 1], norm_pre[l, 1], wp, tabs)
        ya = _attention(_moba_kernel, qa, ka, vat, (sel,),
                        [pl.BlockSpec((PAIR, s // t, ATT_TILE), lambda p, i: (p, 0, i))], LANES, "moba_attention")
        yb = _attention(_sb_kernel, qb, kb, vbt, (upper,), [_const_spec((t, t))], LANES, "stick_breaking_attention",
                        score_scratch=False)
        yc = _attention(_mla_kernel, qc, kc, vct, (), [], PAIR * HEAD_PAD, "mla_attention")
        xs = _merge(xs, ya, yb, yc, gt, wp["w_bt"], wp["w_ot"], mod[l, 1], norm_post[l, 1])
        xs = _ffn(xs, mod[l, 2], norm_pre[l, 2], norm_post[l, 2], ffn_w_gate[l, 1].astype(BF16),
                  ffn_w_up[l, 1].astype(BF16), ffn_w_down[l, 1].astype(BF16), 0.5)
    return xs.reshape(bsz, s, d)
```

```python
import functools

import jax
import jax.numpy as jnp
from jax import lax
from jax.experimental import pallas as pl
from jax.experimental.pallas import tpu as pltpu

F32 = jnp.float32
BF16 = jnp.bfloat16

D_MODEL = 1024
N_HEADS = 8
HEAD_DIM = 64
MOBA_BLOCK = 256
MOBA_TOPK = 3
MLA_Q_LORA = 256
MLA_KV_LORA = 128
MLA_NOPE = 64
MLA_ROPE = 32
MLA_V = 64
MLA_QK = MLA_NOPE + MLA_ROPE
ROPE_THETA = 500000.0
PARTIAL_ROT = HEAD_DIM // 4
D_FF = 2816
N_BRANCH = 3
BRANCH_W = N_HEADS * HEAD_DIM
N_SUB = 3
EPS = 1e-6

LANES = 128
V7X_VMEM_LIMIT = 56 * 1024 * 1024

SEQ_TILE = MOBA_BLOCK
FFN_TILE = 512
MASKED = -1e30
HEAD_PAD = LANES


def _dot(a, b):
    return jnp.dot(a, b, preferred_element_type=F32)


def _dot_nt(a, b, precision=None):
    return lax.dot_general(a, b, (((1,), (1,)), ((), ())), precision=precision, preferred_element_type=F32)


def _rms_norm(x, g):
    return x * lax.rsqrt(jnp.mean(x * x, axis=-1, keepdims=True) + EPS) * g


def _const_spec(shape):
    nd = len(shape)
    return pl.BlockSpec(shape, lambda *_: (0,) * nd, pipeline_mode=pl.Buffered(1))


def _params(n_grid):
    return pltpu.CompilerParams(dimension_semantics=("arbitrary",) * n_grid, vmem_limit_bytes=V7X_VMEM_LIMIT)


def _mod_kernel(c_ref, w_ref, b_ref, o_ref):
    c = c_ref[...]
    ca = c * jax.nn.sigmoid(c)
    o_ref[0] = jnp.dot(ca, w_ref[0], precision=lax.Precision.HIGHEST, preferred_element_type=F32) + b_ref[0]


def _modulation(c, ada_w, ada_b):
    depth, d, n = ada_w.shape
    tn = n // 8
    c8 = jnp.broadcast_to(c.reshape(1, d), (8, d))
    out = pl.pallas_call(
        _mod_kernel,
        grid=(depth, n // tn),
        in_specs=[
            pl.BlockSpec((8, d), lambda l, j: (0, 0)),
            pl.BlockSpec((1, d, tn), lambda l, j: (l, 0, j)),
            pl.BlockSpec((1, 1, tn), lambda l, j: (l, 0, j)),
        ],
        out_specs=pl.BlockSpec((1, 8, tn), lambda l, j: (l, 0, j)),
        out_shape=jax.ShapeDtypeStruct((depth, 8, n), F32),
        compiler_params=_params(2),
        name="adaln_mod",
    )(c8, ada_w, ada_b.reshape(depth, 1, n))
    return out[:, 0, :].reshape(depth, N_SUB, 3, d)


def _ffn_kernel(x_ref, mod_ref, gpre_ref, gpost_ref, wg_ref, wu_ref, wd_ref, o_ref, *, res_w):
    x = x_ref[...]
    shift, scale, gate = mod_ref[0:1, :], mod_ref[1:2, :], mod_ref[2:3, :]
    h = (_rms_norm(x, gpre_ref[...]) * (1.0 + scale) + shift).astype(BF16)
    g = _dot(h, wg_ref[...])
    u = _dot(h, wu_ref[...])
    a = (g * jax.nn.sigmoid(g) * u).astype(BF16)
    y = _dot(a, wd_ref[...])
    o_ref[...] = x + (res_w * (1.0 + gate)) * _rms_norm(y, gpost_ref[...])


def _ffn(x, mod, g_pre, g_post, wg, wu, wd, res_w):
    s, d = x.shape
    tm = min(FFN_TILE, s)
    dff = wg.shape[1]
    return pl.pallas_call(
        functools.partial(_ffn_kernel, res_w=res_w),
        grid=(s // tm,),
        in_specs=[
            pl.BlockSpec((tm, d), lambda i: (i, 0)),
            _const_spec((3, d)),
            _const_spec((1, d)),
            _const_spec((1, d)),
            _const_spec((d, dff)),
            _const_spec((d, dff)),
            _const_spec((dff, d)),
        ],
        out_specs=pl.BlockSpec((tm, d), lambda i: (i, 0)),
        out_shape=jax.ShapeDtypeStruct((s, d), F32),
        compiler_params=_params(1),
        name="macaron_ffn",
    )(x, mod, g_pre.reshape(1, d), g_post.reshape(1, d), wg, wu, wd)


def _rope_rows(x, a, b, c, half):
    outs = []
    for g in range(x.shape[1] // LANES):
        xg = x[:, g * LANES:(g + 1) * LANES]
        outs.append(xg * a + pltpu.roll(xg, LANES - half, 1) * b + pltpu.roll(xg, half, 1) * c)
    return jnp.concatenate(outs, axis=1) if len(outs) > 1 else outs[0]


def _proj_kernel(x_ref, mod_ref, gpre_ref, wrow_ref, wt_ref, wuq_ref, wuk_ref, wuvt_ref, qn_ref, kvn_ref, bg_ref,
                 pa_ref, pb_ref, pc_ref, ma_ref, mb_ref, mc_ref,
                 qa_ref, ka_ref, vat_ref, sel_ref, qb_ref, kb_ref, vbt_ref, qc_ref, kc_ref, vct_ref, gt_ref,
                 kmean_ref, *, n_blocks):
    i = pl.program_id(0)
    w = BRANCH_W
    x = x_ref[...]
    shift, scale = mod_ref[0:1, :], mod_ref[1:2, :]
    hn = (_rms_norm(x, gpre_ref[...]) * (1.0 + scale) + shift).astype(BF16)
    rows = _dot(hn, wrow_ref[...])
    cols_t = _dot_nt(wt_ref[...], hn)

    pa, pb, pc = pa_ref[...], pb_ref[...], pc_ref[...]
    q_a = _rope_rows(rows[:, 0:w], pa, pb, pc, PARTIAL_ROT // 2)
    k_a = _rope_rows(rows[:, w:2 * w], pa, pb, pc, PARTIAL_ROT // 2)
    qa_ref[...] = (q_a * (HEAD_DIM ** -0.5 * LOG2E)).astype(BF16)
    ka_ref[...] = k_a.astype(BF16)
    vat_ref[0] = cols_t[0:w, :].astype(BF16)

    @pl.when(i == 0)
    def _():
        kmean_ref[...] = jnp.zeros_like(kmean_ref)

    kmean_ref[pl.ds(i, 1), :] = jnp.mean(k_a, axis=0, keepdims=True)
    kmean = kmean_ref[...]
    t = x.shape[0]
    lane = lax.broadcasted_iota(jnp.int32, (1, LANES), 1)
    blk = lax.broadcasted_iota(jnp.int32, (n_blocks, t), 0)
    past = blk < i
    blk_f = blk.astype(F32)
    for h in range(N_HEADS):
        p, e = h // 2, h % 2
        in_head = (lane >= HEAD_DIM) if e else (lane < HEAD_DIM)
        qm = jnp.where(in_head, q_a[:, p * LANES:(p + 1) * LANES], 0.0)
        gate = _dot_nt(kmean[:, p * LANES:(p + 1) * LANES], qm, precision=lax.Precision.HIGHEST)
        gate = jnp.where(past, gate, -jnp.inf)
        chosen = jnp.zeros(gate.shape, F32)
        for _ in range(min(MOBA_TOPK, n_blocks)):
            top = jnp.max(gate, axis=0, keepdims=True)
            first = jnp.min(jnp.where(gate == top, blk_f, float(n_blocks)), axis=0, keepdims=True)
            pick = blk_f == first
            chosen = jnp.where(pick, 1.0, chosen)
            gate = jnp.where(pick, -jnp.inf, gate)
        sel_ref[h] = jnp.where(jnp.logical_and(chosen > 0.0, past), 0.0, MASKED)

    qb_ref[...] = (rows[:, 2 * w:3 * w] * HEAD_DIM ** -0.5).astype(BF16)
    kb_ref[...] = rows[:, 3 * w:4 * w].astype(BF16)
    vbt_ref[0] = cols_t[w:2 * w, :].astype(BF16)

    o = 4 * w
    ma, mb, mc = ma_ref[...], mb_ref[...], mc_ref[...]
    cq = _rms_norm(rows[:, o:o + MLA_Q_LORA], qn_ref[...]).astype(BF16)
    q_c = _rope_rows(_dot(cq, wuq_ref[...]), ma, mb, mc, MLA_ROPE // 2)
    qc_ref[...] = (q_c * (MLA_QK ** -0.5 * LOG2E)).astype(BF16)
    o += MLA_Q_LORA
    ckv = _rms_norm(rows[:, o:o + MLA_KV_LORA], kvn_ref[...]).astype(BF16)
    o += MLA_KV_LORA
    k_rope = _rope_rows(rows[:, o:o + HEAD_PAD], ma, mb, mc, MLA_ROPE // 2)
    kc_ref[...] = (_dot(ckv, wuk_ref[...]) + jnp.concatenate([k_rope] * N_HEADS, axis=1)).astype(BF16)
    vct_ref[0] = _dot_nt(wuvt_ref[...], ckv).astype(BF16)

    gt_ref[...] = jax.nn.sigmoid(cols_t[2 * w:, :] + bg_ref[...]).astype(BF16)


def _mixer_proj(x, mod, g_pre, wp, tabs):
    s, d = x.shape
    t = SEQ_TILE
    nb = s // t
    w = BRANCH_W
    hp = N_HEADS * HEAD_PAD
    row_spec = lambda n: pl.BlockSpec((t, n), lambda i: (i, 0))
    vt_spec = pl.BlockSpec((1, w, t), lambda i: (i, 0, 0))
    tab_spec = pl.BlockSpec((t, LANES), lambda i: (i, 0))
    out_shape = (
        jax.ShapeDtypeStruct((s, w), BF16), jax.ShapeDtypeStruct((s, w), BF16), jax.ShapeDtypeStruct((nb, w, t), BF16),
        jax.ShapeDtypeStruct((N_HEADS, nb, s), F32),
        jax.ShapeDtypeStruct((s, w), BF16), jax.ShapeDtypeStruct((s, w), BF16), jax.ShapeDtypeStruct((nb, w, t), BF16),
        jax.ShapeDtypeStruct((s, hp), BF16), jax.ShapeDtypeStruct((s, hp), BF16), jax.ShapeDtypeStruct((nb, w, t), BF16),
        jax.ShapeDtypeStruct((N_BRANCH * d, s), BF16),
    )
    out_specs = (
        row_spec(w), row_spec(w), vt_spec,
        pl.BlockSpec((N_HEADS, nb, t), lambda i: (0, 0, i)),
        row_spec(w), row_spec(w), vt_spec,
        row_spec(hp), row_spec(hp), vt_spec,
        pl.BlockSpec((N_BRANCH * d, t), lambda i: (0, i)),
    )
    in_specs = [
        pl.BlockSpec((t, d), lambda i: (i, 0)),
        _const_spec((3, d)), _const_spec((1, d)),
        _const_spec(wp["w_row"].shape), _const_spec(wp["w_t"].shape), _const_spec(wp["w_uq"].shape),
        _const_spec(wp["w_uk"].shape), _const_spec(wp["w_uvt"].shape),
        _const_spec((1, MLA_Q_LORA)), _const_spec((1, MLA_KV_LORA)), _const_spec((N_BRANCH * d, 1)),
    ] + [tab_spec] * 6
    return pl.pallas_call(
        functools.partial(_proj_kernel, n_blocks=nb),
        grid=(nb,),
        in_specs=in_specs,
        out_specs=out_specs,
        out_shape=out_shape,
        scratch_shapes=[pltpu.VMEM((nb, w), F32)],
        compiler_params=_params(1),
        name="mixer_proj",
    )(x, mod, g_pre.reshape(1, d), wp["w_row"], wp["w_t"], wp["w_uq"], wp["w_uk"], wp["w_uvt"],
      wp["q_norm"], wp["kv_norm"], wp["b_gate"], *tabs)


ATT_TILE = 512
SLAB_CHUNKS = ATT_TILE // SEQ_TILE
PAIR = 2
ONES_ROWS = 16
LOG2E = 1.4426950408889634
SB_EXP_IS_ZERO = -104.0


def _key_rows(c):
    return pl.ds(pl.multiple_of(c * SEQ_TILE, SEQ_TILE), SEQ_TILE)


def _packed_pair(q_ref, k_ref):
    q = q_ref[...]
    lane = lax.broadcasted_iota(jnp.int32, (1, LANES), 1)
    qs = [jnp.where((lane >= HEAD_DIM) if e else (lane < HEAD_DIM), q, jnp.zeros_like(q)) for e in range(PAIR)]
    return qs, lambda c, e: k_ref[_key_rows(c), :]


def _wide_pair(q_ref, k_ref):
    qs = [q_ref[:, e * LANES:(e + 1) * LANES] for e in range(PAIR)]
    return qs, lambda c, e: k_ref[_key_rows(c), e * LANES:(e + 1) * LANES]


def _value_t(vt_ref, c, e):
    return vt_ref[c, e * HEAD_DIM:(e + 1) * HEAD_DIM, :]


def _diag_positions(c, q_start=0):
    key = lax.broadcasted_iota(jnp.int32, (SEQ_TILE, ATT_TILE), 0) + c * SEQ_TILE
    qry = lax.broadcasted_iota(jnp.int32, (SEQ_TILE, ATT_TILE), 1) + q_start
    return key, qry


def _softmax_attention(q_pair, vt_ref, o_ref, s_refs, diag_mask, past_offset):
    qs, keys = q_pair
    i = pl.program_id(1)
    own, q_start = i, 0
    ones = jnp.ones((ONES_ROWS, SEQ_TILE), BF16)

    def score(e, j, diagonal):
        col_max, offsets = None, []
        for c in range(SLAB_CHUNKS):
            cg = j * SLAB_CHUNKS + c
            s = _dot_nt(keys(cg, e), qs[e])
            off = None
            if diagonal:
                s = diag_mask(e, c, cg, s, q_start)
            elif past_offset is not None:
                off = past_offset(e, cg)
            s_refs[e][c * SEQ_TILE:(c + 1) * SEQ_TILE, :] = s
            cm = jnp.max(s, axis=0, keepdims=True)
            cm = cm if off is None else cm + off
            col_max = cm if col_max is None else jnp.maximum(col_max, cm)
            offsets.append(off)
        return col_max, offsets

    def update(e, j, carry, scored):
        m, acc = carry
        col_max, offsets = scored
        m_new = jnp.maximum(m, col_max)
        acc = jnp.exp2(m - m_new) * acc
        for c in range(SLAB_CHUNKS):
            shift = m_new if offsets[c] is None else m_new - offsets[c]
            p = jnp.exp2(s_refs[e][c * SEQ_TILE:(c + 1) * SEQ_TILE, :] - shift)
            v_ones = jnp.concatenate([_value_t(vt_ref, j * SLAB_CHUNKS + c, e), ones], axis=0)
            acc = acc + _dot(v_ones, p.astype(BF16))
        return m_new, acc

    t = ATT_TILE
    init = (jnp.full((1, t), MASKED, F32), jnp.zeros((HEAD_DIM + ONES_ROWS, t), F32))
    last = jnp.maximum(own - 1, 0)
    scored0 = score(0, own, True)
    scored1 = score(1, own, True)
    c0 = update(0, own, init, scored0)
    scored0 = score(0, 0, False)
    c1 = update(1, own, init, scored1)

    def past_slab(j, state):
        c0, c1, scored0 = state
        scored1 = score(1, j, False)
        c0 = update(0, j, c0, scored0)
        scored0 = score(0, jnp.minimum(j + 1, last), False)
        c1 = update(1, j, c1, scored1)
        return c0, c1, scored0

    c0, c1, _ = lax.fori_loop(0, own, past_slab, (c0, c1, scored0))
    for e, (_, acc) in enumerate((c0, c1)):
        o_ref[e * HEAD_DIM:(e + 1) * HEAD_DIM, :] = (acc[:HEAD_DIM] / acc[HEAD_DIM:HEAD_DIM + 1]).astype(o_ref.dtype)


def _moba_kernel(q_ref, k_ref, vt_ref, sel_ref, o_ref, s0_ref, s1_ref):
    def selected(e, cg):
        return sel_ref[e, pl.ds(cg, 1), :]

    def diag_mask(e, c, cg, s, q_start):
        key, qry = _diag_positions(c, q_start)
        own = lax.shift_right_logical(qry, MOBA_BLOCK.bit_length() - 1) == c
        return jnp.where(own, jnp.where(key <= qry, s, MASKED), s + selected(e, cg))

    _softmax_attention(_packed_pair(q_ref, k_ref), vt_ref, o_ref, (s0_ref, s1_ref), diag_mask, selected)


def _mla_kernel(q_ref, k_ref, vt_ref, o_ref, s0_ref, s1_ref):
    def diag_mask(e, c, cg, s, q_start):
        key, qry = _diag_positions(c, q_start)
        return jnp.where(key <= qry, s, MASKED)

    _softmax_attention(_wide_pair(q_ref, k_ref), vt_ref, o_ref, (s0_ref, s1_ref), diag_mask, None)


def _log1m_beta(z):
    return -(jnp.maximum(z, 0.0) + jnp.log1p(jnp.exp(-jnp.abs(z))))


def _sb_kernel(q_ref, k_ref, vt_ref, up_ref, o_ref, a_ref, hi_ref, lo_ref):
    i = pl.program_id(0)
    t = SEQ_TILE
    upper = up_ref[...]
    lane = lax.broadcasted_iota(jnp.int32, (1, LANES), 1)
    qs = []
    for h in range(N_HEADS):
        q = q_ref[:, (h // PAIR) * LANES:(h // PAIR + 1) * LANES]
        qs.append(jnp.where((lane >= HEAD_DIM) if h % PAIR else (lane < HEAD_DIM), q, jnp.zeros_like(q)))

    def score(h, j, keep):
        z = _dot_nt(k_ref[_key_rows(j), (h // PAIR) * LANES:(h // PAIR + 1) * LANES], qs[h])
        log1m = _log1m_beta(z)
        if keep is not None:
            log1m = jnp.where(keep, log1m, 0.0)
        hi = log1m.astype(BF16)
        a_ref[h] = z + log1m
        hi_ref[h] = hi
        lo_ref[h] = (log1m - hi.astype(F32)).astype(BF16)
        return jnp.sum(log1m, axis=0, keepdims=True)

    def weigh(h, j, carry, col_sum, keep):
        c, acc = carry
        suffix = _dot(upper, hi_ref[h]) + _dot(upper, lo_ref[h]) + c
        wgt = jnp.exp(a_ref[h] + suffix)
        if keep is not None:
            wgt = jnp.where(keep, wgt, 0.0)
        return c + col_sum, acc + _dot(_value_t(vt_ref, j, h), wgt.astype(BF16))

    def chunk(j, carry, diagonal):
        keep = None
        if diagonal:
            keep = lax.broadcasted_iota(jnp.int32, (t, t), 0) < lax.broadcasted_iota(jnp.int32, (t, t), 1)
        out = []
        col_sum = score(0, j, keep)
        for h in range(N_HEADS):
            nxt = score(h + 1, j, keep) if h + 1 < N_HEADS else None
            out.append(weigh(h, j, carry[h], col_sum, keep))
            col_sum = nxt
        return tuple(out)

    init = ((jnp.zeros((1, t), F32), jnp.zeros((HEAD_DIM, t), F32)),) * N_HEADS
    carry = chunk(i, init, True)

    def alive(state):
        n, carry = state
        c_max = carry[0][0]
        for h in range(1, N_HEADS):
            c_max = jnp.maximum(c_max, carry[h][0])
        return jnp.logical_and(n < i, jnp.max(c_max) > SB_EXP_IS_ZERO)

    _, carry = lax.while_loop(alive, lambda st: (st[0] + 1, chunk(i - 1 - st[0], st[1], False)), (0, carry))
    for h in range(N_HEADS):
        o_ref[h * HEAD_DIM:(h + 1) * HEAD_DIM, :] = carry[h][1].astype(o_ref.dtype)


def _stick_breaking(q, k, vt, upper):
    s, w = q.shape
    t = SEQ_TILE
    return pl.pallas_call(
        _sb_kernel,
        grid=(s // t,),
        in_specs=[pl.BlockSpec((t, w), lambda i: (i, 0)), _const_spec((s, w)), _const_spec((s // t, w, t)),
                  _const_spec((t, t))],
        out_specs=pl.BlockSpec((w, t), lambda i: (0, i)),
        out_shape=jax.ShapeDtypeStruct((w, s), BF16),
        scratch_shapes=[pltpu.VMEM((N_HEADS, t, t), F32), pltpu.VMEM((N_HEADS, t, t), BF16),
                        pltpu.VMEM((N_HEADS, t, t), BF16)],
        compiler_params=_params(1),
        name="stick_breaking_attention",
    )(q, k, vt, upper)


def _attention(kernel, q, k, vt, extra, extra_specs, pair_lanes, name):
    q_tile = ATT_TILE
    scratch = [pltpu.VMEM((ATT_TILE, q_tile), F32)] * PAIR
    s = q.shape[0]
    nb = s // SEQ_TILE
    rows = PAIR * HEAD_DIM
    return pl.pallas_call(
        kernel,
        grid=(N_HEADS // PAIR, s // q_tile),
        in_specs=[
            pl.BlockSpec((q_tile, pair_lanes), lambda p, i: (i, p)),
            pl.BlockSpec((s, pair_lanes), lambda p, i: (0, p)),
            pl.BlockSpec((nb, rows, SEQ_TILE), lambda p, i: (0, p, 0)),
        ] + extra_specs,
        out_specs=pl.BlockSpec((rows, q_tile), lambda p, i: (p, i)),
        out_shape=jax.ShapeDtypeStruct((N_HEADS * HEAD_DIM, s), BF16),
        scratch_shapes=scratch,
        compiler_params=_params(2),
        name=name,
    )(q, k, vt, *extra)


def _merge_kernel(x_ref, ya_ref, yb_ref, yc_ref, gt_ref, wbt_ref, wot_ref, mod_ref, gpost_ref, o_ref):
    d = D_MODEL
    merged_t = None
    for n, y_ref in enumerate((ya_ref, yb_ref, yc_ref)):
        part = _dot(wbt_ref[n], y_ref[...]) * gt_ref[n * d:(n + 1) * d, :].astype(F32)
        merged_t = part if merged_t is None else merged_t + part
    out = _dot(wot_ref[...], merged_t.astype(BF16)).T
    gate = mod_ref[2:3, :]
    o_ref[...] = x_ref[...] + (1.0 + gate) * _rms_norm(out, gpost_ref[...])


def _merge(x, ya, yb, yc, gt, wbt, wot, mod, g_post):
    s, d = x.shape
    t = SEQ_TILE
    y_spec = pl.BlockSpec((BRANCH_W, t), lambda i: (0, i))
    return pl.pallas_call(
        _merge_kernel,
        grid=(s // t,),
        in_specs=[
            pl.BlockSpec((t, d), lambda i: (i, 0)),
            y_spec, y_spec, y_spec,
            pl.BlockSpec((N_BRANCH * d, t), lambda i: (0, i)),
            _const_spec(wbt.shape), _const_spec(wot.shape), _const_spec((3, d)), _const_spec((1, d)),
        ],
        out_specs=pl.BlockSpec((t, d), lambda i: (i, 0)),
        out_shape=jax.ShapeDtypeStruct((s, d), F32),
        compiler_params=_params(1),
        name="mixer_merge",
    )(x, ya, yb, yc, gt, wbt, wot, mod, g_post.reshape(1, d))


def _rope_tables(n_rot, seq, first_lane, period):
    half = n_rot // 2
    inv = jnp.power(jnp.float32(ROPE_THETA), -jnp.arange(0, n_rot, 2, dtype=F32) / n_rot)
    ang = jnp.arange(seq, dtype=F32)[:, None] * inv[None, :]
    cos, sin = jnp.cos(ang), jnp.sin(ang)
    lane = jnp.arange(LANES) % period - first_lane
    lo = (lane >= 0) & (lane < half)
    hi = (lane >= half) & (lane < n_rot)
    idx = jnp.clip(lane, 0, n_rot - 1) % half
    a = jnp.where((lo | hi)[None, :], cos[:, idx], 1.0)
    b = jnp.where(lo[None, :], -sin[:, idx], 0.0)
    c = jnp.where(hi[None, :], sin[:, idx], 0.0)
    return a, b, c


def _pad_heads(w, n_heads, width):
    k = w.shape[0]
    w = w.reshape(k, n_heads, width)
    return jnp.pad(w, ((0, 0), (0, 0), (0, HEAD_PAD - width))).reshape(k, n_heads * HEAD_PAD)


def _mixer_weights(w_in, b_gate, q_norm, w_uq, kv_norm, w_ukv, w_branch, w_out):
    w = BRANCH_W
    d = D_MODEL
    qa, ka, va, qb, kb, vb = (w_in[:, n * w:(n + 1) * w] for n in range(6))
    o = 6 * w
    w_cq = w_in[:, o:o + MLA_Q_LORA]
    o += MLA_Q_LORA
    w_ckv = w_in[:, o:o + MLA_KV_LORA]
    o += MLA_KV_LORA
    w_kr = jnp.pad(w_in[:, o:o + MLA_ROPE], ((0, 0), (MLA_NOPE, HEAD_PAD - MLA_QK)))
    o += MLA_ROPE
    w_gates = w_in[:, o:]
    ukv = w_ukv.reshape(MLA_KV_LORA, N_HEADS, MLA_NOPE + MLA_V)
    return {
        "w_row": jnp.concatenate([qa, ka, qb, kb, w_cq, w_ckv, w_kr], axis=1).astype(BF16),
        "w_t": jnp.concatenate([va, vb, w_gates], axis=1).T.astype(BF16),
        "w_uq": _pad_heads(w_uq, N_HEADS, MLA_QK).astype(BF16),
        "w_uk": _pad_heads(ukv[:, :, :MLA_NOPE].reshape(MLA_KV_LORA, -1), N_HEADS, MLA_NOPE).astype(BF16),
        "w_uvt": ukv[:, :, MLA_NOPE:].reshape(MLA_KV_LORA, -1).T.astype(BF16),
        "q_norm": q_norm.reshape(1, -1),
        "kv_norm": kv_norm.reshape(1, -1),
        "b_gate": b_gate.reshape(N_BRANCH * d, 1),
        "w_bt": jnp.swapaxes(w_branch, 1, 2).astype(BF16),
        "w_ot": w_out.T.astype(BF16),
    }


def kernel(x, c, ada_w, ada_b, norm_pre, norm_post, ffn_w_gate, ffn_w_up, ffn_w_down, mix_w_in, mix_b_gate,
           mla_q_norm, mla_w_uq, mla_kv_norm, mla_w_ukv, mix_w_branch, mix_w_out):
    bsz, s, d = x.shape
    assert bsz == 1 and d == D_MODEL and s % ATT_TILE == 0 and s % min(FFN_TILE, s) == 0
    depth = ada_w.shape[0]
    t = SEQ_TILE
    mod = _modulation(c, ada_w, ada_b)
    tabs = _rope_tables(PARTIAL_ROT, s, 0, HEAD_DIM) + _rope_tables(MLA_ROPE, s, MLA_NOPE, HEAD_PAD)
    upper = (jnp.arange(t)[None, :] > jnp.arange(t)[:, None]).astype(BF16)
    xs = x.reshape(s, d)
    for l in range(depth):
        xs = _ffn(xs, mod[l, 0], norm_pre[l, 0], norm_post[l, 0], ffn_w_gate[l, 0].astype(BF16),
                  ffn_w_up[l, 0].astype(BF16), ffn_w_down[l, 0].astype(BF16), 0.5)
        wp = _mixer_weights(mix_w_in[l], mix_b_gate[l], mla_q_norm[l], mla_w_uq[l], mla_kv_norm[l], mla_w_ukv[l],
                            mix_w_branch[l], mix_w_out[l])
        qa, ka, vat, sel, qb, kb, vbt, qc, kc, vct, gt = _mixer_proj(xs, mod[l, 1], norm_pre[l, 1], wp, tabs)
        ya = _attention(_moba_kernel, qa, ka, vat, (sel,),
                        [pl.BlockSpec((PAIR, s // t, ATT_TILE), lambda p, i: (p, 0, i))], LANES, "moba_attention")
        yb = _stick_breaking(qb, kb, vbt, upper)
        yc = _attention(_mla_kernel, qc, kc, vct, (), [], PAIR * HEAD_PAD, "mla_attention")
        xs = _merge(xs, ya, yb, yc, gt, wp["w_bt"], wp["w_ot"], mod[l, 1], norm_post[l, 1])
        xs = _ffn(xs, mod[l, 2], norm_pre[l, 2], norm_post[l, 2], ffn_w_gate[l, 1].astype(BF16),
                  ffn_w_up[l, 1].astype(BF16), ffn_w_down[l, 1].astype(BF16), 0.5)
    return xs.reshape(bsz, s, d)
```

```python
import functools

import jax
import jax.numpy as jnp
from jax import lax
from jax.experimental import pallas as pl
from jax.experimental.pallas import tpu as pltpu

F32 = jnp.float32
BF16 = jnp.bfloat16

D_MODEL = 1024
N_HEADS = 8
HEAD_DIM = 64
MOBA_BLOCK = 256
MOBA_TOPK = 3
MLA_Q_LORA = 256
MLA_KV_LORA = 128
MLA_NOPE = 64
MLA_ROPE = 32
MLA_V = 64
MLA_QK = MLA_NOPE + MLA_ROPE
ROPE_THETA = 500000.0
PARTIAL_ROT = HEAD_DIM // 4
D_FF = 2816
N_BRANCH = 3
BRANCH_W = N_HEADS * HEAD_DIM
N_SUB = 3
EPS = 1e-6

LANES = 128
V7X_VMEM_LIMIT = 56 * 1024 * 1024

SEQ_TILE = MOBA_BLOCK
FFN_TILE = 512
MASKED = -1e30
HEAD_PAD = LANES


def _dot(a, b):
    return jnp.dot(a, b, preferred_element_type=F32)


def _dot_nt(a, b, precision=None):
    return lax.dot_general(a, b, (((1,), (1,)), ((), ())), precision=precision, preferred_element_type=F32)


def _rms_norm(x, g):
    return x * lax.rsqrt(jnp.mean(x * x, axis=-1, keepdims=True) + EPS) * g


def _const_spec(shape):
    nd = len(shape)
    return pl.BlockSpec(shape, lambda *_: (0,) * nd, pipeline_mode=pl.Buffered(1))


def _params(n_grid):
    return pltpu.CompilerParams(dimension_semantics=("arbitrary",) * n_grid, vmem_limit_bytes=V7X_VMEM_LIMIT)


def _mod_kernel(c_ref, w_ref, b_ref, o_ref):
    c = c_ref[...]
    ca = c * jax.nn.sigmoid(c)
    o_ref[0] = jnp.dot(ca, w_ref[0], precision=lax.Precision.HIGHEST, preferred_element_type=F32) + b_ref[0]


def _modulation(c, ada_w, ada_b):
    depth, d, n = ada_w.shape
    tn = n // 8
    c8 = jnp.broadcast_to(c.reshape(1, d), (8, d))
    out = pl.pallas_call(
        _mod_kernel,
        grid=(depth, n // tn),
        in_specs=[
            pl.BlockSpec((8, d), lambda l, j: (0, 0)),
            pl.BlockSpec((1, d, tn), lambda l, j: (l, 0, j)),
            pl.BlockSpec((1, 1, tn), lambda l, j: (l, 0, j)),
        ],
        out_specs=pl.BlockSpec((1, 8, tn), lambda l, j: (l, 0, j)),
        out_shape=jax.ShapeDtypeStruct((depth, 8, n), F32),
        compiler_params=_params(2),
        name="adaln_mod",
    )(c8, ada_w, ada_b.reshape(depth, 1, n))
    return out[:, 0, :].reshape(depth, N_SUB, 3, d)


def _ffn_kernel(x_ref, mod_ref, gpre_ref, gpost_ref, wg_ref, wu_ref, wd_ref, o_ref, *, res_w):
    x = x_ref[...]
    shift, scale, gate = mod_ref[0:1, :], mod_ref[1:2, :], mod_ref[2:3, :]
    h = (_rms_norm(x, gpre_ref[...]) * (1.0 + scale) + shift).astype(BF16)
    g = _dot(h, wg_ref[...])
    u = _dot(h, wu_ref[...])
    a = (g * jax.nn.sigmoid(g) * u).astype(BF16)
    y = _dot(a, wd_ref[...])
    o_ref[...] = x + (res_w * (1.0 + gate)) * _rms_norm(y, gpost_ref[...])


def _ffn(x, mod, g_pre, g_post, wg, wu, wd, res_w):
    s, d = x.shape
    tm = min(FFN_TILE, s)
    dff = wg.shape[1]
    return pl.pallas_call(
        functools.partial(_ffn_kernel, res_w=res_w),
        grid=(s // tm,),
        in_specs=[
            pl.BlockSpec((tm, d), lambda i: (i, 0)),
            _const_spec((3, d)),
            _const_spec((1, d)),
            _const_spec((1, d)),
            _const_spec((d, dff)),
            _const_spec((d, dff)),
            _const_spec((dff, d)),
        ],
        out_specs=pl.BlockSpec((tm, d), lambda i: (i, 0)),
        out_shape=jax.ShapeDtypeStruct((s, d), F32),
        compiler_params=_params(1),
        name="macaron_ffn",
    )(x, mod, g_pre.reshape(1, d), g_post.reshape(1, d), wg, wu, wd)


def _rope_rows(x, a, b, c, half):
    outs = []
    for g in range(x.shape[1] // LANES):
        xg = x[:, g * LANES:(g + 1) * LANES]
        outs.append(xg * a + pltpu.roll(xg, LANES - half, 1) * b + pltpu.roll(xg, half, 1) * c)
    return jnp.concatenate(outs, axis=1) if len(outs) > 1 else outs[0]


def _proj_kernel(x_ref, mod_ref, gpre_ref, wrow_ref, wt_ref, wuq_ref, wuk_ref, wuvt_ref, qn_ref, kvn_ref, bg_ref,
                 pa_ref, pb_ref, pc_ref, ma_ref, mb_ref, mc_ref,
                 qa_ref, ka_ref, vat_ref, sel_ref, qb_ref, kb_ref, vbt_ref, qc_ref, kc_ref, vct_ref, gt_ref,
                 kmean_ref, *, n_blocks):
    i = pl.program_id(0)
    w = BRANCH_W
    x = x_ref[...]
    shift, scale = mod_ref[0:1, :], mod_ref[1:2, :]
    hn = (_rms_norm(x, gpre_ref[...]) * (1.0 + scale) + shift).astype(BF16)
    rows = _dot(hn, wrow_ref[...])
    cols_t = _dot_nt(wt_ref[...], hn)

    pa, pb, pc = pa_ref[...], pb_ref[...], pc_ref[...]
    q_a = _rope_rows(rows[:, 0:w], pa, pb, pc, PARTIAL_ROT // 2)
    k_a = _rope_rows(rows[:, w:2 * w], pa, pb, pc, PARTIAL_ROT // 2)
    qa_ref[...] = (q_a * (HEAD_DIM ** -0.5 * LOG2E)).astype(BF16)
    ka_ref[...] = k_a.astype(BF16)
    vat_ref[0] = cols_t[0:w, :].astype(BF16)

    @pl.when(i == 0)
    def _():
        kmean_ref[...] = jnp.zeros_like(kmean_ref)

    kmean_ref[pl.ds(i, 1), :] = jnp.mean(k_a, axis=0, keepdims=True)
    kmean = kmean_ref[...]
    t = x.shape[0]
    lane = lax.broadcasted_iota(jnp.int32, (1, LANES), 1)
    blk = lax.broadcasted_iota(jnp.int32, (n_blocks, t), 0)
    past = blk < i
    blk_f = blk.astype(F32)
    for h in range(N_HEADS):
        p, e = h // 2, h % 2
        in_head = (lane >= HEAD_DIM) if e else (lane < HEAD_DIM)
        qm = jnp.where(in_head, q_a[:, p * LANES:(p + 1) * LANES], 0.0)
        gate = _dot_nt(kmean[:, p * LANES:(p + 1) * LANES], qm, precision=lax.Precision.HIGHEST)
        gate = jnp.where(past, gate, -jnp.inf)
        chosen = jnp.zeros(gate.shape, F32)
        for _ in range(min(MOBA_TOPK, n_blocks)):
            top = jnp.max(gate, axis=0, keepdims=True)
            first = jnp.min(jnp.where(gate == top, blk_f, float(n_blocks)), axis=0, keepdims=True)
            pick = blk_f == first
            chosen = jnp.where(pick, 1.0, chosen)
            gate = jnp.where(pick, -jnp.inf, gate)
        sel_ref[h] = jnp.where(jnp.logical_and(chosen > 0.0, past), 0.0, MASKED)

    qb_ref[...] = (rows[:, 2 * w:3 * w] * HEAD_DIM ** -0.5).astype(BF16)
    kb_ref[...] = rows[:, 3 * w:4 * w].astype(BF16)
    vbt_ref[0] = cols_t[w:2 * w, :].astype(BF16)

    o = 4 * w
    ma, mb, mc = ma_ref[...], mb_ref[...], mc_ref[...]
    cq = _rms_norm(rows[:, o:o + MLA_Q_LORA], qn_ref[...]).astype(BF16)
    q_c = _rope_rows(_dot(cq, wuq_ref[...]), ma, mb, mc, MLA_ROPE // 2)
    qc_ref[...] = (q_c * (MLA_QK ** -0.5 * LOG2E)).astype(BF16)
    o += MLA_Q_LORA
    ckv = _rms_norm(rows[:, o:o + MLA_KV_LORA], kvn_ref[...]).astype(BF16)
    o += MLA_KV_LORA
    k_rope = _rope_rows(rows[:, o:o + HEAD_PAD], ma, mb, mc, MLA_ROPE // 2)
    kc_ref[...] = (_dot(ckv, wuk_ref[...]) + jnp.concatenate([k_rope] * N_HEADS, axis=1)).astype(BF16)
    vct_ref[0] = _dot_nt(wuvt_ref[...], ckv).astype(BF16)

    gt_ref[...] = jax.nn.sigmoid(cols_t[2 * w:, :] + bg_ref[...]).astype(BF16)


def _mixer_proj(x, mod, g_pre, wp, tabs):
    s, d = x.shape
    t = SEQ_TILE
    nb = s // t
    w = BRANCH_W
    hp = N_HEADS * HEAD_PAD
    row_spec = lambda n: pl.BlockSpec((t, n), lambda i: (i, 0))
    vt_spec = pl.BlockSpec((1, w, t), lambda i: (i, 0, 0))
    tab_spec = pl.BlockSpec((t, LANES), lambda i: (i, 0))
    out_shape = (
        jax.ShapeDtypeStruct((s, w), BF16), jax.ShapeDtypeStruct((s, w), BF16), jax.ShapeDtypeStruct((nb, w, t), BF16),
        jax.ShapeDtypeStruct((N_HEADS, nb, s), F32),
        jax.ShapeDtypeStruct((s, w), BF16), jax.ShapeDtypeStruct((s, w), BF16), jax.ShapeDtypeStruct((nb, w, t), BF16),
        jax.ShapeDtypeStruct((s, hp), BF16), jax.ShapeDtypeStruct((s, hp), BF16), jax.ShapeDtypeStruct((nb, w, t), BF16),
        jax.ShapeDtypeStruct((N_BRANCH * d, s), BF16),
    )
    out_specs = (
        row_spec(w), row_spec(w), vt_spec,
        pl.BlockSpec((N_HEADS, nb, t), lambda i: (0, 0, i)),
        row_spec(w), row_spec(w), vt_spec,
        row_spec(hp), row_spec(hp), vt_spec,
        pl.BlockSpec((N_BRANCH * d, t), lambda i: (0, i)),
    )
    in_specs = [
        pl.BlockSpec((t, d), lambda i: (i, 0)),
        _const_spec((3, d)), _const_spec((1, d)),
        _const_spec(wp["w_row"].shape), _const_spec(wp["w_t"].shape), _const_spec(wp["w_uq"].shape),
        _const_spec(wp["w_uk"].shape), _const_spec(wp["w_uvt"].shape),
        _const_spec((1, MLA_Q_LORA)), _const_spec((1, MLA_KV_LORA)), _const_spec((N_BRANCH * d, 1)),
    ] + [tab_spec] * 6
    return pl.pallas_call(
        functools.partial(_proj_kernel, n_blocks=nb),
        grid=(nb,),
        in_specs=in_specs,
        out_specs=out_specs,
        out_shape=out_shape,
        scratch_shapes=[pltpu.VMEM((nb, w), F32)],
        compiler_params=_params(1),
        name="mixer_proj",
    )(x, mod, g_pre.reshape(1, d), wp["w_row"], wp["w_t"], wp["w_uq"], wp["w_uk"], wp["w_uvt"],
      wp["q_norm"], wp["kv_norm"], wp["b_gate"], *tabs)


ATT_TILE = 512
SLAB_CHUNKS = ATT_TILE // SEQ_TILE
PAIR = 2
MOBA_GROUP = 8
MLA_GROUP = 4
ONES_ROWS = 16
LOG2E = 1.4426950408889634
SB_EXP_IS_ZERO = -104.0


def _key_rows(c):
    return pl.ds(pl.multiple_of(c * SEQ_TILE, SEQ_TILE), SEQ_TILE)


def _slab_rows(j):
    return pl.ds(pl.multiple_of(j * ATT_TILE, ATT_TILE), ATT_TILE)


def _packed_heads(q_ref, k_ref, group):
    lane = lax.broadcasted_iota(jnp.int32, (1, LANES), 1)
    qs = []
    for e in range(group):
        q = q_ref[:, (e // PAIR) * LANES:(e // PAIR + 1) * LANES]
        qs.append(jnp.where((lane >= HEAD_DIM) if e % PAIR else (lane < HEAD_DIM), q, jnp.zeros_like(q)))
    return qs, lambda j, e: k_ref[_slab_rows(j), (e // PAIR) * LANES:(e // PAIR + 1) * LANES]


def _wide_heads(q_ref, k_ref, group):
    qs = [q_ref[:, e * LANES:(e + 1) * LANES] for e in range(group)]
    return qs, lambda j, e: k_ref[_slab_rows(j), e * LANES:(e + 1) * LANES]


def _value_t(vt_ref, c, e):
    return vt_ref[c, e * HEAD_DIM:(e + 1) * HEAD_DIM, :]


def _diag_positions(c):
    key = lax.broadcasted_iota(jnp.int32, (SEQ_TILE, ATT_TILE), 0) + c * SEQ_TILE
    qry = lax.broadcasted_iota(jnp.int32, (SEQ_TILE, ATT_TILE), 1)
    return key, qry


def _softmax_attention(q_pair, vt_ref, o_ref, s_refs, diag_mask, past_offset):
    qs, keys = q_pair
    group = len(qs)
    own = pl.program_id(1)
    ones = jnp.ones((ONES_ROWS, SEQ_TILE), BF16)

    def score(e, j, diagonal):
        col_max, offsets = None, []
        slab = _dot_nt(keys(j, e), qs[e])
        for c in range(SLAB_CHUNKS):
            cg = j * SLAB_CHUNKS + c
            s = slab[c * SEQ_TILE:(c + 1) * SEQ_TILE, :]
            off = None
            if diagonal:
                s = diag_mask(e, c, cg, s)
            elif past_offset is not None:
                off = past_offset(e, cg)
            s_refs[e][c * SEQ_TILE:(c + 1) * SEQ_TILE, :] = s
            cm = jnp.max(s, axis=0, keepdims=True)
            cm = cm if off is None else cm + off
            col_max = cm if col_max is None else jnp.maximum(col_max, cm)
            offsets.append(off)
        return col_max, offsets

    def update(e, j, carry, scored):
        m, acc = carry
        col_max, offsets = scored
        m_new = jnp.maximum(m, col_max)
        acc = jnp.exp2(m - m_new) * acc
        for c in range(SLAB_CHUNKS):
            shift = m_new if offsets[c] is None else m_new - offsets[c]
            p = jnp.exp2(s_refs[e][c * SEQ_TILE:(c + 1) * SEQ_TILE, :] - shift)
            v_ones = jnp.concatenate([_value_t(vt_ref, j * SLAB_CHUNKS + c, e), ones], axis=0)
            acc = acc + _dot(v_ones, p.astype(BF16))
        return m_new, acc

    t = ATT_TILE
    init = (jnp.full((1, t), MASKED, F32), jnp.zeros((HEAD_DIM + ONES_ROWS, t), F32))
    last = jnp.maximum(own - 1, 0)

    def ring(j, carries, scored, diagonal, j_ahead):
        carries = list(carries)
        for e in range(group):
            ahead = score(e + 1, j, diagonal) if e + 1 < group else score(0, j_ahead, False)
            carries[e] = update(e, j, carries[e], scored)
            scored = ahead
        return tuple(carries), scored

    state = ring(own, (init,) * group, score(0, own, True), True, 0)
    carries, _ = lax.fori_loop(
        0, own, lambda j, st: ring(j, st[0], st[1], False, jnp.minimum(j + 1, last)), state)
    for e, (_, acc) in enumerate(carries):
        o_ref[e * HEAD_DIM:(e + 1) * HEAD_DIM, :] = (acc[:HEAD_DIM] / acc[HEAD_DIM:HEAD_DIM + 1]).astype(o_ref.dtype)


def _moba_kernel(q_ref, k_ref, vt_ref, sel_ref, o_ref, *s_refs):
    def selected(e, cg):
        return sel_ref[e, pl.ds(cg, 1), :]

    def diag_mask(e, c, cg, s):
        key, qry = _diag_positions(c)
        own = lax.shift_right_logical(qry, MOBA_BLOCK.bit_length() - 1) == c
        return jnp.where(own, jnp.where(key <= qry, s, MASKED), s + selected(e, cg))

    _softmax_attention(_packed_heads(q_ref, k_ref, len(s_refs)), vt_ref, o_ref, s_refs, diag_mask, selected)


def _mla_kernel(q_ref, k_ref, vt_ref, o_ref, *s_refs):
    def diag_mask(e, c, cg, s):
        key, qry = _diag_positions(c)
        return jnp.where(key <= qry, s, MASKED)

    _softmax_attention(_wide_heads(q_ref, k_ref, len(s_refs)), vt_ref, o_ref, s_refs, diag_mask, None)


def _log1m_beta(z):
    return -(jnp.maximum(z, 0.0) + jnp.log1p(jnp.exp(-jnp.abs(z))))


def _sb_kernel(q_ref, k_ref, vt_ref, up_ref, o_ref, a_ref, hi_ref, lo_ref):
    i = pl.program_id(0)
    t = SEQ_TILE
    upper = up_ref[...]
    lane = lax.broadcasted_iota(jnp.int32, (1, LANES), 1)
    qs = []
    for h in range(N_HEADS):
        q = q_ref[:, (h // PAIR) * LANES:(h // PAIR + 1) * LANES]
        qs.append(jnp.where((lane >= HEAD_DIM) if h % PAIR else (lane < HEAD_DIM), q, jnp.zeros_like(q)))

    def score(h, j, keep):
        z = _dot_nt(k_ref[_key_rows(j), (h // PAIR) * LANES:(h // PAIR + 1) * LANES], qs[h])
        log1m = _log1m_beta(z)
        if keep is not None:
            log1m = jnp.where(keep, log1m, 0.0)
        hi = log1m.astype(BF16)
        a_ref[h] = z + log1m
        hi_ref[h] = hi
        lo_ref[h] = (log1m - hi.astype(F32)).astype(BF16)
        return jnp.sum(log1m, axis=0, keepdims=True)

    def weigh(h, j, carry, col_sum, keep):
        c, acc = carry
        suffix = _dot(upper, hi_ref[h]) + _dot(upper, lo_ref[h]) + c
        wgt = jnp.exp(a_ref[h] + suffix)
        if keep is not None:
            wgt = jnp.where(keep, wgt, 0.0)
        return c + col_sum, acc + _dot(_value_t(vt_ref, j, h), wgt.astype(BF16))

    def chunk(j, carry, diagonal):
        keep = None
        if diagonal:
            keep = lax.broadcasted_iota(jnp.int32, (t, t), 0) < lax.broadcasted_iota(jnp.int32, (t, t), 1)
        out = []
        col_sum = score(0, j, keep)
        for h in range(N_HEADS):
            nxt = score(h + 1, j, keep) if h + 1 < N_HEADS else None
            out.append(weigh(h, j, carry[h], col_sum, keep))
            col_sum = nxt
        return tuple(out)

    init = ((jnp.zeros((1, t), F32), jnp.zeros((HEAD_DIM, t), F32)),) * N_HEADS
    carry = chunk(i, init, True)

    def alive(state):
        n, carry = state
        c_max = carry[0][0]
        for h in range(1, N_HEADS):
            c_max = jnp.maximum(c_max, carry[h][0])
        return jnp.logical_and(n < i, jnp.max(c_max) > SB_EXP_IS_ZERO)

    _, carry = lax.while_loop(alive, lambda st: (st[0] + 1, chunk(i - 1 - st[0], st[1], False)), (0, carry))
    for h in range(N_HEADS):
        o_ref[h * HEAD_DIM:(h + 1) * HEAD_DIM, :] = carry[h][1].astype(o_ref.dtype)


def _stick_breaking(q, k, vt, upper):
    s, w = q.shape
    t = SEQ_TILE
    return pl.pallas_call(
        _sb_kernel,
        grid=(s // t,),
        in_specs=[pl.BlockSpec((t, w), lambda i: (i, 0)), _const_spec((s, w)), _const_spec((s // t, w, t)),
                  _const_spec((t, t))],
        out_specs=pl.BlockSpec((w, t), lambda i: (0, i)),
        out_shape=jax.ShapeDtypeStruct((w, s), BF16),
        scratch_shapes=[pltpu.VMEM((N_HEADS, t, t), F32), pltpu.VMEM((N_HEADS, t, t), BF16),
                        pltpu.VMEM((N_HEADS, t, t), BF16)],
        compiler_params=_params(1),
        name="stick_breaking_attention",
    )(q, k, vt, upper)


def _attention(kernel, q, k, vt, extra, extra_specs, group, group_lanes, name):
    s = q.shape[0]
    nb = s // SEQ_TILE
    rows = group * HEAD_DIM
    return pl.pallas_call(
        kernel,
        grid=(N_HEADS // group, s // ATT_TILE),
        in_specs=[
            pl.BlockSpec((ATT_TILE, group_lanes), lambda g, i: (i, g)),
            pl.BlockSpec((s, group_lanes), lambda g, i: (0, g), pipeline_mode=pl.Buffered(1)),
            pl.BlockSpec((nb, rows, SEQ_TILE), lambda g, i: (0, g, 0), pipeline_mode=pl.Buffered(1)),
        ] + extra_specs,
        out_specs=pl.BlockSpec((rows, ATT_TILE), lambda g, i: (g, i)),
        out_shape=jax.ShapeDtypeStruct((N_HEADS * HEAD_DIM, s), BF16),
        scratch_shapes=[pltpu.VMEM((ATT_TILE, ATT_TILE), F32)] * group,
        compiler_params=_params(2),
        name=name,
    )(q, k, vt, *extra)


def _merge_kernel(x_ref, ya_ref, yb_ref, yc_ref, gt_ref, wbt_ref, wot_ref, mod_ref, gpost_ref, o_ref):
    d = D_MODEL
    merged_t = None
    for n, y_ref in enumerate((ya_ref, yb_ref, yc_ref)):
        part = _dot(wbt_ref[n], y_ref[...]) * gt_ref[n * d:(n + 1) * d, :].astype(F32)
        merged_t = part if merged_t is None else merged_t + part
    out = _dot(wot_ref[...], merged_t.astype(BF16)).T
    gate = mod_ref[2:3, :]
    o_ref[...] = x_ref[...] + (1.0 + gate) * _rms_norm(out, gpost_ref[...])


def _merge(x, ya, yb, yc, gt, wbt, wot, mod, g_post):
    s, d = x.shape
    t = SEQ_TILE
    y_spec = pl.BlockSpec((BRANCH_W, t), lambda i: (0, i))
    return pl.pallas_call(
        _merge_kernel,
        grid=(s // t,),
        in_specs=[
            pl.BlockSpec((t, d), lambda i: (i, 0)),
            y_spec, y_spec, y_spec,
            pl.BlockSpec((N_BRANCH * d, t), lambda i: (0, i)),
            _const_spec(wbt.shape), _const_spec(wot.shape), _const_spec((3, d)), _const_spec((1, d)),
        ],
        out_specs=pl.BlockSpec((t, d), lambda i: (i, 0)),
        out_shape=jax.ShapeDtypeStruct((s, d), F32),
        compiler_params=_params(1),
        name="mixer_merge",
    )(x, ya, yb, yc, gt, wbt, wot, mod, g_post.reshape(1, d))


def _rope_tables(n_rot, seq, first_lane, period):
    half = n_rot // 2
    inv = jnp.power(jnp.float32(ROPE_THETA), -jnp.arange(0, n_rot, 2, dtype=F32) / n_rot)
    ang = jnp.arange(seq, dtype=F32)[:, None] * inv[None, :]
    cos, sin = jnp.cos(ang), jnp.sin(ang)
    lane = jnp.arange(LANES) % period - first_lane
    lo = (lane >= 0) & (lane < half)
    hi = (lane >= half) & (lane < n_rot)
    idx = jnp.clip(lane, 0, n_rot - 1) % half
    a = jnp.where((lo | hi)[None, :], cos[:, idx], 1.0)
    b = jnp.where(lo[None, :], -sin[:, idx], 0.0)
    c = jnp.where(hi[None, :], sin[:, idx], 0.0)
    return a, b, c


def _pad_heads(w, n_heads, width):
    k = w.shape[0]
    w = w.reshape(k, n_heads, width)
    return jnp.pad(w, ((0, 0), (0, 0), (0, HEAD_PAD - width))).reshape(k, n_heads * HEAD_PAD)


def _mixer_weights(w_in, b_gate, q_norm, w_uq, kv_norm, w_ukv, w_branch, w_out):
    w = BRANCH_W
    d = D_MODEL
    qa, ka, va, qb, kb, vb = (w_in[:, n * w:(n + 1) * w] for n in range(6))
    o = 6 * w
    w_cq = w_in[:, o:o + MLA_Q_LORA]
    o += MLA_Q_LORA
    w_ckv = w_in[:, o:o + MLA_KV_LORA]
    o += MLA_KV_LORA
    w_kr = jnp.pad(w_in[:, o:o + MLA_ROPE], ((0, 0), (MLA_NOPE, HEAD_PAD - MLA_QK)))
    o += MLA_ROPE
    w_gates = w_in[:, o:]
    ukv = w_ukv.reshape(MLA_KV_LORA, N_HEADS, MLA_NOPE + MLA_V)
    return {
        "w_row": jnp.concatenate([qa, ka, qb, kb, w_cq, w_ckv, w_kr], axis=1).astype(BF16),
        "w_t": jnp.concatenate([va, vb, w_gates], axis=1).T.astype(BF16),
        "w_uq": _pad_heads(w_uq, N_HEADS, MLA_QK).astype(BF16),
        "w_uk": _pad_heads(ukv[:, :, :MLA_NOPE].reshape(MLA_KV_LORA, -1), N_HEADS, MLA_NOPE).astype(BF16),
        "w_uvt": ukv[:, :, MLA_NOPE:].reshape(MLA_KV_LORA, -1).T.astype(BF16),
        "q_norm": q_norm.reshape(1, -1),
        "kv_norm": kv_norm.reshape(1, -1),
        "b_gate": b_gate.reshape(N_BRANCH * d, 1),
        "w_bt": jnp.swapaxes(w_branch, 1, 2).astype(BF16),
        "w_ot": w_out.T.astype(BF16),
    }


def kernel(x, c, ada_w, ada_b, norm_pre, norm_post, ffn_w_gate, ffn_w_up, ffn_w_down, mix_w_in, mix_b_gate,
           mla_q_norm, mla_w_uq, mla_kv_norm, mla_w_ukv, mix_w_branch, mix_w_out):
    bsz, s, d = x.shape
    assert bsz == 1 and d == D_MODEL and s % ATT_TILE == 0 and s % min(FFN_TILE, s) == 0
    depth = ada_w.shape[0]
    t = SEQ_TILE
    mod = _modulation(c, ada_w, ada_b)
    tabs = _rope_tables(PARTIAL_ROT, s, 0, HEAD_DIM) + _rope_tables(MLA_ROPE, s, MLA_NOPE, HEAD_PAD)
    upper = (jnp.arange(t)[None, :] > jnp.arange(t)[:, None]).astype(BF16)
    xs = x.reshape(s, d)
    for l in range(depth):
        xs = _ffn(xs, mod[l, 0], norm_pre[l, 0], norm_post[l, 0], ffn_w_gate[l, 0].astype(BF16),
                  ffn_w_up[l, 0].astype(BF16), ffn_w_down[l, 0].astype(BF16), 0.5)
        wp = _mixer_weights(mix_w_in[l], mix_b_gate[l], mla_q_norm[l], mla_w_uq[l], mla_kv_norm[l], mla_w_ukv[l],
                            mix_w_branch[l], mix_w_out[l])
        qa, ka, vat, sel, qb, kb, vbt, qc, kc, vct, gt = _mixer_proj(xs, mod[l, 1], norm_pre[l, 1], wp, tabs)
        ya = _attention(_moba_kernel, qa, ka, vat, (sel,),
                        [pl.BlockSpec((MOBA_GROUP, s // t, ATT_TILE), lambda g, i: (g, 0, i))], MOBA_GROUP,
                        MOBA_GROUP // PAIR * LANES, "moba_attention")
        yb = _stick_breaking(qb, kb, vbt, upper)
        yc = _attention(_mla_kernel, qc, kc, vct, (), [], MLA_GROUP, MLA_GROUP * HEAD_PAD, "mla_attention")
        xs = _merge(xs, ya, yb, yc, gt, wp["w_bt"], wp["w_ot"], mod[l, 1], norm_post[l, 1])
        xs = _ffn(xs, mod[l, 2], norm_pre[l, 2], norm_post[l, 2], ffn_w_gate[l, 1].astype(BF16),
                  ffn_w_up[l, 1].astype(BF16), ffn_w_down[l, 1].astype(BF16), 0.5)
    return xs.reshape(bsz, s, d)
```

```python
import functools

import jax
import jax.numpy as jnp
from jax import lax
from jax.experimental import pallas as pl
from jax.experimental.pallas import tpu as pltpu

F32 = jnp.float32
BF16 = jnp.bfloat16

D_MODEL = 1024
N_HEADS = 8
HEAD_DIM = 64
MOBA_BLOCK = 256
MOBA_TOPK = 3
MLA_Q_LORA = 256
MLA_KV_LORA = 128
MLA_NOPE = 64
MLA_ROPE = 32
MLA_V = 64
MLA_QK = MLA_NOPE + MLA_ROPE
ROPE_THETA = 500000.0
PARTIAL_ROT = HEAD_DIM // 4
D_FF = 2816
N_BRANCH = 3
BRANCH_W = N_HEADS * HEAD_DIM
N_SUB = 3
EPS = 1e-6

LANES = 128
V7X_VMEM_LIMIT = 56 * 1024 * 1024

SEQ_TILE = MOBA_BLOCK
FFN_TILE = 512
MASKED = -1e30
HEAD_PAD = LANES


def _dot(a, b):
    return jnp.dot(a, b, preferred_element_type=F32)


def _dot_nt(a, b, precision=None):
    return lax.dot_general(a, b, (((1,), (1,)), ((), ())), precision=precision, preferred_element_type=F32)


def _rms_norm(x, g):
    return x * lax.rsqrt(jnp.mean(x * x, axis=-1, keepdims=True) + EPS) * g


def _const_spec(shape):
    nd = len(shape)
    return pl.BlockSpec(shape, lambda *_: (0,) * nd, pipeline_mode=pl.Buffered(1))


def _params(n_grid):
    return pltpu.CompilerParams(dimension_semantics=("arbitrary",) * n_grid, vmem_limit_bytes=V7X_VMEM_LIMIT)


def _mod_kernel(c_ref, w_ref, b_ref, o_ref):
    c = c_ref[...]
    ca = c * jax.nn.sigmoid(c)
    o_ref[0] = jnp.dot(ca, w_ref[0], precision=lax.Precision.HIGHEST, preferred_element_type=F32) + b_ref[0]


def _modulation(c, ada_w, ada_b):
    depth, d, n = ada_w.shape
    tn = n // 8
    c8 = jnp.broadcast_to(c.reshape(1, d), (8, d))
    out = pl.pallas_call(
        _mod_kernel,
        grid=(depth, n // tn),
        in_specs=[
            pl.BlockSpec((8, d), lambda l, j: (0, 0)),
            pl.BlockSpec((1, d, tn), lambda l, j: (l, 0, j)),
            pl.BlockSpec((1, 1, tn), lambda l, j: (l, 0, j)),
        ],
        out_specs=pl.BlockSpec((1, 8, tn), lambda l, j: (l, 0, j)),
        out_shape=jax.ShapeDtypeStruct((depth, 8, n), F32),
        compiler_params=_params(2),
        name="adaln_mod",
    )(c8, ada_w, ada_b.reshape(depth, 1, n))
    return out[:, 0, :].reshape(depth, N_SUB, 3, d)


def _ffn_kernel(x_ref, mod_ref, gpre_ref, gpost_ref, wg_ref, wu_ref, wd_ref, o_ref, *, res_w):
    x = x_ref[...]
    shift, scale, gate = mod_ref[0:1, :], mod_ref[1:2, :], mod_ref[2:3, :]
    h = (_rms_norm(x, gpre_ref[...]) * (1.0 + scale) + shift).astype(BF16)
    g = _dot(h, wg_ref[...])
    u = _dot(h, wu_ref[...])
    a = (g * jax.nn.sigmoid(g) * u).astype(BF16)
    y = _dot(a, wd_ref[...])
    o_ref[...] = x + (res_w * (1.0 + gate)) * _rms_norm(y, gpost_ref[...])


def _ffn(x, mod, g_pre, g_post, wg, wu, wd, res_w):
    s, d = x.shape
    tm = min(FFN_TILE, s)
    dff = wg.shape[1]
    return pl.pallas_call(
        functools.partial(_ffn_kernel, res_w=res_w),
        grid=(s // tm,),
        in_specs=[
            pl.BlockSpec((tm, d), lambda i: (i, 0)),
            _const_spec((3, d)),
            _const_spec((1, d)),
            _const_spec((1, d)),
            _const_spec((d, dff)),
            _const_spec((d, dff)),
            _const_spec((dff, d)),
        ],
        out_specs=pl.BlockSpec((tm, d), lambda i: (i, 0)),
        out_shape=jax.ShapeDtypeStruct((s, d), F32),
        compiler_params=_params(1),
        name="macaron_ffn",
    )(x, mod, g_pre.reshape(1, d), g_post.reshape(1, d), wg, wu, wd)


def _rope_rows(x, a, b, c, half):
    outs = []
    for g in range(x.shape[1] // LANES):
        xg = x[:, g * LANES:(g + 1) * LANES]
        outs.append(xg * a + pltpu.roll(xg, LANES - half, 1) * b + pltpu.roll(xg, half, 1) * c)
    return jnp.concatenate(outs, axis=1) if len(outs) > 1 else outs[0]


def _proj_kernel(x_ref, mod_ref, gpre_ref, wrow_ref, wt_ref, wuq_ref, wuk_ref, wuvt_ref, qn_ref, kvn_ref, bg_ref,
                 pa_ref, pb_ref, pc_ref, ma_ref, mb_ref, mc_ref,
                 qa_ref, ka_ref, vat_ref, sel_ref, qb_ref, kb_ref, vbt_ref, qc_ref, kc_ref, vct_ref, gt_ref,
                 kmean_ref, *, n_blocks):
    i = pl.program_id(0)
    w = BRANCH_W
    x = x_ref[...]
    shift, scale = mod_ref[0:1, :], mod_ref[1:2, :]
    hn = (_rms_norm(x, gpre_ref[...]) * (1.0 + scale) + shift).astype(BF16)
    rows = _dot(hn, wrow_ref[...])
    cols_t = _dot_nt(wt_ref[...], hn)

    pa, pb, pc = pa_ref[...], pb_ref[...], pc_ref[...]
    q_a = _rope_rows(rows[:, 0:w], pa, pb, pc, PARTIAL_ROT // 2)
    k_a = _rope_rows(rows[:, w:2 * w], pa, pb, pc, PARTIAL_ROT // 2)
    qa_ref[...] = (q_a * (HEAD_DIM ** -0.5 * LOG2E)).astype(BF16)
    ka_ref[...] = k_a.astype(BF16)
    vat_ref[0] = cols_t[0:w, :].astype(BF16)

    @pl.when(i == 0)
    def _():
        kmean_ref[...] = jnp.zeros_like(kmean_ref)

    kmean_ref[pl.ds(i, 1), :] = jnp.mean(k_a, axis=0, keepdims=True)
    kmean = kmean_ref[...]
    t = x.shape[0]
    lane = lax.broadcasted_iota(jnp.int32, (1, LANES), 1)
    blk = lax.broadcasted_iota(jnp.int32, (n_blocks, t), 0)
    past = blk < i
    blk_f = blk.astype(F32)
    for h in range(N_HEADS):
        p, e = h // 2, h % 2
        in_head = (lane >= HEAD_DIM) if e else (lane < HEAD_DIM)
        qm = jnp.where(in_head, q_a[:, p * LANES:(p + 1) * LANES], 0.0)
        gate = _dot_nt(kmean[:, p * LANES:(p + 1) * LANES], qm, precision=lax.Precision.HIGHEST)
        gate = jnp.where(past, gate, -jnp.inf)
        chosen = jnp.zeros(gate.shape, F32)
        for _ in range(min(MOBA_TOPK, n_blocks)):
            top = jnp.max(gate, axis=0, keepdims=True)
            first = jnp.min(jnp.where(gate == top, blk_f, float(n_blocks)), axis=0, keepdims=True)
            pick = blk_f == first
            chosen = jnp.where(pick, 1.0, chosen)
            gate = jnp.where(pick, -jnp.inf, gate)
        sel_ref[h] = jnp.where(jnp.logical_and(chosen > 0.0, past), 0.0, MASKED)

    qb_ref[...] = (rows[:, 2 * w:3 * w] * HEAD_DIM ** -0.5).astype(BF16)
    kb_ref[...] = rows[:, 3 * w:4 * w].astype(BF16)
    vbt_ref[0] = cols_t[w:2 * w, :].astype(BF16)

    o = 4 * w
    ma, mb, mc = ma_ref[...], mb_ref[...], mc_ref[...]
    cq = _rms_norm(rows[:, o:o + MLA_Q_LORA], qn_ref[...]).astype(BF16)
    q_c = _rope_rows(_dot(cq, wuq_ref[...]), ma, mb, mc, MLA_ROPE // 2)
    qc_ref[...] = (q_c * (MLA_QK ** -0.5 * LOG2E)).astype(BF16)
    o += MLA_Q_LORA
    ckv = _rms_norm(rows[:, o:o + MLA_KV_LORA], kvn_ref[...]).astype(BF16)
    o += MLA_KV_LORA
    k_rope = _rope_rows(rows[:, o:o + HEAD_PAD], ma, mb, mc, MLA_ROPE // 2)
    kc_ref[...] = (_dot(ckv, wuk_ref[...]) + jnp.concatenate([k_rope] * N_HEADS, axis=1)).astype(BF16)
    vct_ref[0] = _dot_nt(wuvt_ref[...], ckv).astype(BF16)

    gt_ref[...] = jax.nn.sigmoid(cols_t[2 * w:, :] + bg_ref[...]).astype(BF16)


def _mixer_proj(x, mod, g_pre, wp, tabs):
    s, d = x.shape
    t = SEQ_TILE
    nb = s // t
    w = BRANCH_W
    hp = N_HEADS * HEAD_PAD
    row_spec = lambda n: pl.BlockSpec((t, n), lambda i: (i, 0))
    vt_spec = pl.BlockSpec((1, w, t), lambda i: (i, 0, 0))
    tab_spec = pl.BlockSpec((t, LANES), lambda i: (i, 0))
    out_shape = (
        jax.ShapeDtypeStruct((s, w), BF16), jax.ShapeDtypeStruct((s, w), BF16), jax.ShapeDtypeStruct((nb, w, t), BF16),
        jax.ShapeDtypeStruct((N_HEADS, nb, s), F32),
        jax.ShapeDtypeStruct((s, w), BF16), jax.ShapeDtypeStruct((s, w), BF16), jax.ShapeDtypeStruct((nb, w, t), BF16),
        jax.ShapeDtypeStruct((s, hp), BF16), jax.ShapeDtypeStruct((s, hp), BF16), jax.ShapeDtypeStruct((nb, w, t), BF16),
        jax.ShapeDtypeStruct((N_BRANCH * d, s), BF16),
    )
    out_specs = (
        row_spec(w), row_spec(w), vt_spec,
        pl.BlockSpec((N_HEADS, nb, t), lambda i: (0, 0, i)),
        row_spec(w), row_spec(w), vt_spec,
        row_spec(hp), row_spec(hp), vt_spec,
        pl.BlockSpec((N_BRANCH * d, t), lambda i: (0, i)),
    )
    in_specs = [
        pl.BlockSpec((t, d), lambda i: (i, 0)),
        _const_spec((3, d)), _const_spec((1, d)),
        _const_spec(wp["w_row"].shape), _const_spec(wp["w_t"].shape), _const_spec(wp["w_uq"].shape),
        _const_spec(wp["w_uk"].shape), _const_spec(wp["w_uvt"].shape),
        _const_spec((1, MLA_Q_LORA)), _const_spec((1, MLA_KV_LORA)), _const_spec((N_BRANCH * d, 1)),
    ] + [tab_spec] * 6
    return pl.pallas_call(
        functools.partial(_proj_kernel, n_blocks=nb),
        grid=(nb,),
        in_specs=in_specs,
        out_specs=out_specs,
        out_shape=out_shape,
        scratch_shapes=[pltpu.VMEM((nb, w), F32)],
        compiler_params=_params(1),
        name="mixer_proj",
    )(x, mod, g_pre.reshape(1, d), wp["w_row"], wp["w_t"], wp["w_uq"], wp["w_uk"], wp["w_uvt"],
      wp["q_norm"], wp["kv_norm"], wp["b_gate"], *tabs)


ATT_TILE = 512
SLAB_CHUNKS = ATT_TILE // SEQ_TILE
PAIR = 2
MOBA_GROUP = 8
MLA_GROUP = 4
ONES_ROWS = 16
LOG2E = 1.4426950408889634
SB_EXP_IS_ZERO = -104.0


def _key_rows(c):
    return pl.ds(pl.multiple_of(c * SEQ_TILE, SEQ_TILE), SEQ_TILE)


def _slab_rows(j):
    return pl.ds(pl.multiple_of(j * ATT_TILE, ATT_TILE), ATT_TILE)


def _packed_heads(q_ref, k_ref, group):
    lane = lax.broadcasted_iota(jnp.int32, (1, LANES), 1)
    qs = []
    for e in range(group):
        q = q_ref[:, (e // PAIR) * LANES:(e // PAIR + 1) * LANES]
        qs.append(jnp.where((lane >= HEAD_DIM) if e % PAIR else (lane < HEAD_DIM), q, jnp.zeros_like(q)))
    return qs, lambda j, e: k_ref[_slab_rows(j), (e // PAIR) * LANES:(e // PAIR + 1) * LANES]


def _wide_heads(q_ref, k_ref, group):
    qs = [q_ref[:, e * LANES:(e + 1) * LANES] for e in range(group)]
    return qs, lambda j, e: k_ref[_slab_rows(j), e * LANES:(e + 1) * LANES]


def _value_t(vt_ref, c, e):
    return vt_ref[c, e * HEAD_DIM:(e + 1) * HEAD_DIM, :]


def _diag_positions(c):
    key = lax.broadcasted_iota(jnp.int32, (SEQ_TILE, ATT_TILE), 0) + c * SEQ_TILE
    qry = lax.broadcasted_iota(jnp.int32, (SEQ_TILE, ATT_TILE), 1)
    return key, qry


def _softmax_attention(q_pair, vt_ref, o_ref, s_refs, diag_mask, past_offset):
    qs, keys = q_pair
    group = len(qs)
    own = pl.program_id(1)
    ones = jnp.ones((ONES_ROWS, SEQ_TILE), BF16)

    def score(e, j, diagonal):
        col_max, offsets = None, []
        slab = _dot_nt(keys(j, e), qs[e])
        for c in range(SLAB_CHUNKS):
            cg = j * SLAB_CHUNKS + c
            s = slab[c * SEQ_TILE:(c + 1) * SEQ_TILE, :]
            off = None
            if diagonal:
                s = diag_mask(e, c, cg, s)
            elif past_offset is not None:
                off = past_offset(e, cg)
            s_refs[e][c * SEQ_TILE:(c + 1) * SEQ_TILE, :] = s
            cm = jnp.max(s, axis=0, keepdims=True)
            cm = cm if off is None else cm + off
            col_max = cm if col_max is None else jnp.maximum(col_max, cm)
            offsets.append(off)
        return col_max, offsets

    def update(e, j, carry, scored):
        m, acc = carry
        col_max, offsets = scored
        m_new = jnp.maximum(m, col_max)
        acc = jnp.exp2(m - m_new) * acc
        for c in range(SLAB_CHUNKS):
            shift = m_new if offsets[c] is None else m_new - offsets[c]
            p = jnp.exp2(s_refs[e][c * SEQ_TILE:(c + 1) * SEQ_TILE, :] - shift)
            v_ones = jnp.concatenate([_value_t(vt_ref, j * SLAB_CHUNKS + c, e), ones], axis=0)
            acc = acc + _dot(v_ones, p.astype(BF16))
        return m_new, acc

    t = ATT_TILE
    init = (jnp.full((1, t), MASKED, F32), jnp.zeros((HEAD_DIM + ONES_ROWS, t), F32))
    last = jnp.maximum(own - 1, 0)

    def ring(j, carries, scored, diagonal, j_ahead):
        carries = list(carries)
        for e in range(group):
            ahead = score(e + 1, j, diagonal) if e + 1 < group else score(0, j_ahead, False)
            carries[e] = update(e, j, carries[e], scored)
            scored = ahead
        return tuple(carries), scored

    def past(j, state):
        return ring(j, state[0], state[1], False, jnp.minimum(j + 1, last))

    state = ring(own, (init,) * group, score(0, own, True), True, 0)
    pairs = lax.shift_right_logical(own, 1)
    state = lax.fori_loop(0, pairs, lambda n, st: past(2 * n + 1, past(2 * n, st)), state)
    carries, _ = lax.fori_loop(2 * pairs, own, past, state)
    for e, (_, acc) in enumerate(carries):
        o_ref[e * HEAD_DIM:(e + 1) * HEAD_DIM, :] = (acc[:HEAD_DIM] / acc[HEAD_DIM:HEAD_DIM + 1]).astype(o_ref.dtype)


def _moba_kernel(q_ref, k_ref, vt_ref, sel_ref, o_ref, *s_refs):
    def selected(e, cg):
        return sel_ref[e, pl.ds(cg, 1), :]

    def diag_mask(e, c, cg, s):
        key, qry = _diag_positions(c)
        own = lax.shift_right_logical(qry, MOBA_BLOCK.bit_length() - 1) == c
        return jnp.where(own, jnp.where(key <= qry, s, MASKED), s + selected(e, cg))

    _softmax_attention(_packed_heads(q_ref, k_ref, len(s_refs)), vt_ref, o_ref, s_refs, diag_mask, selected)


def _mla_kernel(q_ref, k_ref, vt_ref, o_ref, *s_refs):
    def diag_mask(e, c, cg, s):
        key, qry = _diag_positions(c)
        return jnp.where(key <= qry, s, MASKED)

    _softmax_attention(_wide_heads(q_ref, k_ref, len(s_refs)), vt_ref, o_ref, s_refs, diag_mask, None)


def _log1m_beta(z):
    return -(jnp.maximum(z, 0.0) + jnp.log1p(jnp.exp(-jnp.abs(z))))


def _sb_kernel(q_ref, k_ref, vt_ref, up_ref, o_ref, a_ref, hi_ref, lo_ref):
    i = pl.program_id(0)
    t = SEQ_TILE
    upper = up_ref[...]
    lane = lax.broadcasted_iota(jnp.int32, (1, LANES), 1)
    qs = []
    for h in range(N_HEADS):
        q = q_ref[:, (h // PAIR) * LANES:(h // PAIR + 1) * LANES]
        qs.append(jnp.where((lane >= HEAD_DIM) if h % PAIR else (lane < HEAD_DIM), q, jnp.zeros_like(q)))

    def score(h, j, keep):
        z = _dot_nt(k_ref[_key_rows(j), (h // PAIR) * LANES:(h // PAIR + 1) * LANES], qs[h])
        log1m = _log1m_beta(z)
        if keep is not None:
            log1m = jnp.where(keep, log1m, 0.0)
        hi = log1m.astype(BF16)
        a_ref[h] = z + log1m
        hi_ref[h] = hi
        lo_ref[h] = (log1m - hi.astype(F32)).astype(BF16)
        return jnp.sum(log1m, axis=0, keepdims=True)

    def weigh(h, j, carry, col_sum, keep):
        c, acc = carry
        suffix = _dot(upper, hi_ref[h]) + _dot(upper, lo_ref[h]) + c
        wgt = jnp.exp(a_ref[h] + suffix)
        if keep is not None:
            wgt = jnp.where(keep, wgt, 0.0)
        return c + col_sum, acc + _dot(_value_t(vt_ref, j, h), wgt.astype(BF16))

    def chunk(j, carry, diagonal):
        keep = None
        if diagonal:
            keep = lax.broadcasted_iota(jnp.int32, (t, t), 0) < lax.broadcasted_iota(jnp.int32, (t, t), 1)
        out = []
        col_sum = score(0, j, keep)
        for h in range(N_HEADS):
            nxt = score(h + 1, j, keep) if h + 1 < N_HEADS else None
            out.append(weigh(h, j, carry[h], col_sum, keep))
            col_sum = nxt
        return tuple(out)

    init = ((jnp.zeros((1, t), F32), jnp.zeros((HEAD_DIM, t), F32)),) * N_HEADS
    carry = chunk(i, init, True)

    def alive(state):
        n, carry = state
        c_max = carry[0][0]
        for h in range(1, N_HEADS):
            c_max = jnp.maximum(c_max, carry[h][0])
        return jnp.logical_and(n < i, jnp.max(c_max) > SB_EXP_IS_ZERO)

    _, carry = lax.while_loop(alive, lambda st: (st[0] + 1, chunk(i - 1 - st[0], st[1], False)), (0, carry))
    for h in range(N_HEADS):
        o_ref[h * HEAD_DIM:(h + 1) * HEAD_DIM, :] = carry[h][1].astype(o_ref.dtype)


def _stick_breaking(q, k, vt, upper):
    s, w = q.shape
    t = SEQ_TILE
    return pl.pallas_call(
        _sb_kernel,
        grid=(s // t,),
        in_specs=[pl.BlockSpec((t, w), lambda i: (i, 0)), _const_spec((s, w)), _const_spec((s // t, w, t)),
                  _const_spec((t, t))],
        out_specs=pl.BlockSpec((w, t), lambda i: (0, i)),
        out_shape=jax.ShapeDtypeStruct((w, s), BF16),
        scratch_shapes=[pltpu.VMEM((N_HEADS, t, t), F32), pltpu.VMEM((N_HEADS, t, t), BF16),
                        pltpu.VMEM((N_HEADS, t, t), BF16)],
        compiler_params=_params(1),
        name="stick_breaking_attention",
    )(q, k, vt, upper)


def _attention(kernel, q, k, vt, extra, extra_specs, group, group_lanes, name):
    s = q.shape[0]
    nb = s // SEQ_TILE
    rows = group * HEAD_DIM
    return pl.pallas_call(
        kernel,
        grid=(N_HEADS // group, s // ATT_TILE),
        in_specs=[
            pl.BlockSpec((ATT_TILE, group_lanes), lambda g, i: (i, g)),
            pl.BlockSpec((s, group_lanes), lambda g, i: (0, g), pipeline_mode=pl.Buffered(1)),
            pl.BlockSpec((nb, rows, SEQ_TILE), lambda g, i: (0, g, 0), pipeline_mode=pl.Buffered(1)),
        ] + extra_specs,
        out_specs=pl.BlockSpec((rows, ATT_TILE), lambda g, i: (g, i)),
        out_shape=jax.ShapeDtypeStruct((N_HEADS * HEAD_DIM, s), BF16),
        scratch_shapes=[pltpu.VMEM((ATT_TILE, ATT_TILE), F32)] * group,
        compiler_params=_params(2),
        name=name,
    )(q, k, vt, *extra)


def _merge_kernel(x_ref, ya_ref, yb_ref, yc_ref, gt_ref, wbt_ref, wot_ref, mod_ref, gpost_ref, o_ref):
    d = D_MODEL
    merged_t = None
    for n, y_ref in enumerate((ya_ref, yb_ref, yc_ref)):
        part = _dot(wbt_ref[n], y_ref[...]) * gt_ref[n * d:(n + 1) * d, :].astype(F32)
        merged_t = part if merged_t is None else merged_t + part
    out = _dot(wot_ref[...], merged_t.astype(BF16)).T
    gate = mod_ref[2:3, :]
    o_ref[...] = x_ref[...] + (1.0 + gate) * _rms_norm(out, gpost_ref[...])


def _merge(x, ya, yb, yc, gt, wbt, wot, mod, g_post):
    s, d = x.shape
    t = ATT_TILE
    y_spec = pl.BlockSpec((BRANCH_W, t), lambda i: (0, i))
    return pl.pallas_call(
        _merge_kernel,
        grid=(s // t,),
        in_specs=[
            pl.BlockSpec((t, d), lambda i: (i, 0)),
            y_spec, y_spec, y_spec,
            pl.BlockSpec((N_BRANCH * d, t), lambda i: (0, i)),
            _const_spec(wbt.shape), _const_spec(wot.shape), _const_spec((3, d)), _const_spec((1, d)),
        ],
        out_specs=pl.BlockSpec((t, d), lambda i: (i, 0)),
        out_shape=jax.ShapeDtypeStruct((s, d), F32),
        compiler_params=_params(1),
        name="mixer_merge",
    )(x, ya, yb, yc, gt, wbt, wot, mod, g_post.reshape(1, d))


def _rope_tables(n_rot, seq, first_lane, period):
    half = n_rot // 2
    inv = jnp.power(jnp.float32(ROPE_THETA), -jnp.arange(0, n_rot, 2, dtype=F32) / n_rot)
    ang = jnp.arange(seq, dtype=F32)[:, None] * inv[None, :]
    cos, sin = jnp.cos(ang), jnp.sin(ang)
    lane = jnp.arange(LANES) % period - first_lane
    lo = (lane >= 0) & (lane < half)
    hi = (lane >= half) & (lane < n_rot)
    idx = jnp.clip(lane, 0, n_rot - 1) % half
    a = jnp.where((lo | hi)[None, :], cos[:, idx], 1.0)
    b = jnp.where(lo[None, :], -sin[:, idx], 0.0)
    c = jnp.where(hi[None, :], sin[:, idx], 0.0)
    return a, b, c


def _pad_heads(w, n_heads, width):
    k = w.shape[0]
    w = w.reshape(k, n_heads, width)
    return jnp.pad(w, ((0, 0), (0, 0), (0, HEAD_PAD - width))).reshape(k, n_heads * HEAD_PAD)


def _mixer_weights(w_in, b_gate, q_norm, w_uq, kv_norm, w_ukv, w_branch, w_out):
    w = BRANCH_W
    d = D_MODEL
    qa, ka, va, qb, kb, vb = (w_in[:, n * w:(n + 1) * w] for n in range(6))
    o = 6 * w
    w_cq = w_in[:, o:o + MLA_Q_LORA]
    o += MLA_Q_LORA
    w_ckv = w_in[:, o:o + MLA_KV_LORA]
    o += MLA_KV_LORA
    w_kr = jnp.pad(w_in[:, o:o + MLA_ROPE], ((0, 0), (MLA_NOPE, HEAD_PAD - MLA_QK)))
    o += MLA_ROPE
    w_gates = w_in[:, o:]
    ukv = w_ukv.reshape(MLA_KV_LORA, N_HEADS, MLA_NOPE + MLA_V)
    return {
        "w_row": jnp.concatenate([qa, ka, qb, kb, w_cq, w_ckv, w_kr], axis=1).astype(BF16),
        "w_t": jnp.concatenate([va, vb, w_gates], axis=1).T.astype(BF16),
        "w_uq": _pad_heads(w_uq, N_HEADS, MLA_QK).astype(BF16),
        "w_uk": _pad_heads(ukv[:, :, :MLA_NOPE].reshape(MLA_KV_LORA, -1), N_HEADS, MLA_NOPE).astype(BF16),
        "w_uvt": ukv[:, :, MLA_NOPE:].reshape(MLA_KV_LORA, -1).T.astype(BF16),
        "q_norm": q_norm.reshape(1, -1),
        "kv_norm": kv_norm.reshape(1, -1),
        "b_gate": b_gate.reshape(N_BRANCH * d, 1),
        "w_bt": jnp.swapaxes(w_branch, 1, 2).astype(BF16),
        "w_ot": w_out.T.astype(BF16),
    }


def kernel(x, c, ada_w, ada_b, norm_pre, norm_post, ffn_w_gate, ffn_w_up, ffn_w_down, mix_w_in, mix_b_gate,
           mla_q_norm, mla_w_uq, mla_kv_norm, mla_w_ukv, mix_w_branch, mix_w_out):
    bsz, s, d = x.shape
    assert bsz == 1 and d == D_MODEL and s % ATT_TILE == 0 and s % min(FFN_TILE, s) == 0
    depth = ada_w.shape[0]
    t = SEQ_TILE
    mod = _modulation(c, ada_w, ada_b)
    tabs = _rope_tables(PARTIAL_ROT, s, 0, HEAD_DIM) + _rope_tables(MLA_ROPE, s, MLA_NOPE, HEAD_PAD)
    upper = (jnp.arange(t)[None, :] > jnp.arange(t)[:, None]).astype(BF16)
    xs = x.reshape(s, d)
    for l in range(depth):
        xs = _ffn(xs, mod[l, 0], norm_pre[l, 0], norm_post[l, 0], ffn_w_gate[l, 0].astype(BF16),
                  ffn_w_up[l, 0].astype(BF16), ffn_w_down[l, 0].astype(BF16), 0.5)
        wp = _mixer_weights(mix_w_in[l], mix_b_gate[l], mla_q_norm[l], mla_w_uq[l], mla_kv_norm[l], mla_w_ukv[l],
                            mix_w_branch[l], mix_w_out[l])
        qa, ka, vat, sel, qb, kb, vbt, qc, kc, vct, gt = _mixer_proj(xs, mod[l, 1], norm_pre[l, 1], wp, tabs)
        ya = _attention(_moba_kernel, qa, ka, vat, (sel,),
                        [pl.BlockSpec((MOBA_GROUP, s // t, ATT_TILE), lambda g, i: (g, 0, i))], MOBA_GROUP,
                        MOBA_GROUP // PAIR * LANES, "moba_attention")
        yb = _stick_breaking(qb, kb, vbt, upper)
        yc = _attention(_mla_kernel, qc, kc, vct, (), [], MLA_GROUP, MLA_GROUP * HEAD_PAD, "mla_attention")
        xs = _merge(xs, ya, yb, yc, gt, wp["w_bt"], wp["w_ot"], mod[l, 1], norm_post[l, 1])
        xs = _ffn(xs, mod[l, 2], norm_pre[l, 2], norm_post[l, 2], ffn_w_gate[l, 1].astype(BF16),
                  ffn_w_up[l, 1].astype(BF16), ffn_w_down[l, 1].astype(BF16), 0.5)
    return xs.reshape(bsz, s, d)
```

```python
import functools

import jax
import jax.numpy as jnp
from jax import lax
from jax.experimental import pallas as pl
from jax.experimental.pallas import tpu as pltpu

F32 = jnp.float32
BF16 = jnp.bfloat16

D_MODEL = 1024
N_HEADS = 8
HEAD_DIM = 64
MOBA_BLOCK = 256
MOBA_TOPK = 3
MLA_Q_LORA = 256
MLA_KV_LORA = 128
MLA_NOPE = 64
MLA_ROPE = 32
MLA_V = 64
MLA_QK = MLA_NOPE + MLA_ROPE
ROPE_THETA = 500000.0
PARTIAL_ROT = HEAD_DIM // 4
D_FF = 2816
N_BRANCH = 3
BRANCH_W = N_HEADS * HEAD_DIM
N_SUB = 3
EPS = 1e-6

LANES = 128
V7X_VMEM_LIMIT = 56 * 1024 * 1024

SEQ_TILE = MOBA_BLOCK
FFN_TILE = 512
MASKED = -1e30
HEAD_PAD = LANES


def _dot(a, b):
    return jnp.dot(a, b, preferred_element_type=F32)


def _dot_nt(a, b, precision=None):
    return lax.dot_general(a, b, (((1,), (1,)), ((), ())), precision=precision, preferred_element_type=F32)


def _rms_norm(x, g):
    return x * lax.rsqrt(jnp.mean(x * x, axis=-1, keepdims=True) + EPS) * g


def _const_spec(shape):
    nd = len(shape)
    return pl.BlockSpec(shape, lambda *_: (0,) * nd, pipeline_mode=pl.Buffered(1))


def _params(n_grid):
    return pltpu.CompilerParams(dimension_semantics=("arbitrary",) * n_grid, vmem_limit_bytes=V7X_VMEM_LIMIT)


def _mod_kernel(c_ref, w_ref, b_ref, o_ref):
    c = c_ref[...]
    ca = c * jax.nn.sigmoid(c)
    o_ref[0] = jnp.dot(ca, w_ref[0], precision=lax.Precision.HIGHEST, preferred_element_type=F32) + b_ref[0]


def _modulation(c, ada_w, ada_b):
    depth, d, n = ada_w.shape
    tn = n // 8
    c8 = jnp.broadcast_to(c.reshape(1, d), (8, d))
    out = pl.pallas_call(
        _mod_kernel,
        grid=(depth, n // tn),
        in_specs=[
            pl.BlockSpec((8, d), lambda l, j: (0, 0)),
            pl.BlockSpec((1, d, tn), lambda l, j: (l, 0, j)),
            pl.BlockSpec((1, 1, tn), lambda l, j: (l, 0, j)),
        ],
        out_specs=pl.BlockSpec((1, 8, tn), lambda l, j: (l, 0, j)),
        out_shape=jax.ShapeDtypeStruct((depth, 8, n), F32),
        compiler_params=_params(2),
        name="adaln_mod",
    )(c8, ada_w, ada_b.reshape(depth, 1, n))
    return out[:, 0, :].reshape(depth, N_SUB, 3, d)


def _ffn_kernel(x_ref, mod_ref, gpre_ref, gpost_ref, wg_ref, wu_ref, wd_ref, o_ref, *, res_w):
    x = x_ref[...]
    shift, scale, gate = mod_ref[0:1, :], mod_ref[1:2, :], mod_ref[2:3, :]
    h = (_rms_norm(x, gpre_ref[...]) * (1.0 + scale) + shift).astype(BF16)
    g = _dot(h, wg_ref[...])
    u = _dot(h, wu_ref[...])
    a = (g * jax.nn.sigmoid(g) * u).astype(BF16)
    y = _dot(a, wd_ref[...])
    o_ref[...] = x + (res_w * (1.0 + gate)) * _rms_norm(y, gpost_ref[...])


def _ffn(x, mod, g_pre, g_post, wg, wu, wd, res_w):
    s, d = x.shape
    tm = min(FFN_TILE, s)
    dff = wg.shape[1]
    return pl.pallas_call(
        functools.partial(_ffn_kernel, res_w=res_w),
        grid=(s // tm,),
        in_specs=[
            pl.BlockSpec((tm, d), lambda i: (i, 0)),
            _const_spec((3, d)),
            _const_spec((1, d)),
            _const_spec((1, d)),
            _const_spec((d, dff)),
            _const_spec((d, dff)),
            _const_spec((dff, d)),
        ],
        out_specs=pl.BlockSpec((tm, d), lambda i: (i, 0)),
        out_shape=jax.ShapeDtypeStruct((s, d), F32),
        compiler_params=_params(1),
        name="macaron_ffn",
    )(x, mod, g_pre.reshape(1, d), g_post.reshape(1, d), wg, wu, wd)


def _rope_rows(x, a, b, c, half):
    outs = []
    for g in range(x.shape[1] // LANES):
        xg = x[:, g * LANES:(g + 1) * LANES]
        outs.append(xg * a + pltpu.roll(xg, LANES - half, 1) * b + pltpu.roll(xg, half, 1) * c)
    return jnp.concatenate(outs, axis=1) if len(outs) > 1 else outs[0]


def _proj_kernel(x_ref, mod_ref, gpre_ref, wrow_ref, wt_ref, wuq_ref, wuk_ref, wuvt_ref, qn_ref, kvn_ref,
                 pa_ref, pb_ref, pc_ref, ma_ref, mb_ref, mc_ref,
                 qa_ref, ka_ref, vat_ref, sel_ref, qb_ref, kb_ref, vbt_ref, qc_ref, kc_ref, vct_ref,
                 kmean_ref, *, n_blocks):
    i = pl.program_id(0)
    w = BRANCH_W
    x = x_ref[...]
    shift, scale = mod_ref[0:1, :], mod_ref[1:2, :]
    hn = (_rms_norm(x, gpre_ref[...]) * (1.0 + scale) + shift).astype(BF16)
    rows = _dot(hn, wrow_ref[...])
    cols_t = _dot_nt(wt_ref[...], hn)

    pa, pb, pc = pa_ref[...], pb_ref[...], pc_ref[...]
    q_a = _rope_rows(rows[:, 0:w], pa, pb, pc, PARTIAL_ROT // 2)
    k_a = _rope_rows(rows[:, w:2 * w], pa, pb, pc, PARTIAL_ROT // 2)
    qa_ref[...] = (q_a * (HEAD_DIM ** -0.5 * LOG2E)).astype(BF16)
    ka_ref[...] = k_a.astype(BF16)
    vat_ref[0] = cols_t[0:w, :].astype(BF16)

    @pl.when(i == 0)
    def _():
        kmean_ref[...] = jnp.zeros_like(kmean_ref)

    kmean_ref[pl.ds(i, 1), :] = jnp.mean(k_a, axis=0, keepdims=True)
    kmean = kmean_ref[...]
    t = x.shape[0]
    lane = lax.broadcasted_iota(jnp.int32, (1, LANES), 1)
    blk = lax.broadcasted_iota(jnp.int32, (n_blocks, t), 0)
    past = blk < i
    blk_f = blk.astype(F32)
    for h in range(N_HEADS):
        p, e = h // 2, h % 2
        in_head = (lane >= HEAD_DIM) if e else (lane < HEAD_DIM)
        qm = jnp.where(in_head, q_a[:, p * LANES:(p + 1) * LANES], 0.0)
        gate = _dot_nt(kmean[:, p * LANES:(p + 1) * LANES], qm, precision=lax.Precision.HIGHEST)
        gate = jnp.where(past, gate, -jnp.inf)
        chosen = jnp.zeros(gate.shape, F32)
        for _ in range(min(MOBA_TOPK, n_blocks)):
            top = jnp.max(gate, axis=0, keepdims=True)
            first = jnp.min(jnp.where(gate == top, blk_f, float(n_blocks)), axis=0, keepdims=True)
            pick = blk_f == first
            chosen = jnp.where(pick, 1.0, chosen)
            gate = jnp.where(pick, -jnp.inf, gate)
        sel_ref[h] = jnp.where(jnp.logical_and(chosen > 0.0, past), 0.0, MASKED)

    qb_ref[...] = (rows[:, 2 * w:3 * w] * HEAD_DIM ** -0.5).astype(BF16)
    kb_ref[...] = rows[:, 3 * w:4 * w].astype(BF16)
    vbt_ref[0] = cols_t[w:2 * w, :].astype(BF16)

    o = 4 * w
    ma, mb, mc = ma_ref[...], mb_ref[...], mc_ref[...]
    cq = _rms_norm(rows[:, o:o + MLA_Q_LORA], qn_ref[...]).astype(BF16)
    q_c = _rope_rows(_dot(cq, wuq_ref[...]), ma, mb, mc, MLA_ROPE // 2)
    qc_ref[...] = (q_c * (MLA_QK ** -0.5 * LOG2E)).astype(BF16)
    o += MLA_Q_LORA
    ckv = _rms_norm(rows[:, o:o + MLA_KV_LORA], kvn_ref[...]).astype(BF16)
    o += MLA_KV_LORA
    k_rope = _rope_rows(rows[:, o:o + HEAD_PAD], ma, mb, mc, MLA_ROPE // 2)
    kc_ref[...] = (_dot(ckv, wuk_ref[...]) + jnp.concatenate([k_rope] * N_HEADS, axis=1)).astype(BF16)
    vct_ref[0] = _dot_nt(wuvt_ref[...], ckv).astype(BF16)


def _mixer_proj(x, mod, g_pre, wp, tabs):
    s, d = x.shape
    t = SEQ_TILE
    nb = s // t
    w = BRANCH_W
    hp = N_HEADS * HEAD_PAD
    row_spec = lambda n: pl.BlockSpec((t, n), lambda i: (i, 0))
    vt_spec = pl.BlockSpec((1, w, t), lambda i: (i, 0, 0))
    tab_spec = pl.BlockSpec((t, LANES), lambda i: (i, 0))
    out_shape = (
        jax.ShapeDtypeStruct((s, w), BF16), jax.ShapeDtypeStruct((s, w), BF16), jax.ShapeDtypeStruct((nb, w, t), BF16),
        jax.ShapeDtypeStruct((N_HEADS, nb, s), F32),
        jax.ShapeDtypeStruct((s, w), BF16), jax.ShapeDtypeStruct((s, w), BF16), jax.ShapeDtypeStruct((nb, w, t), BF16),
        jax.ShapeDtypeStruct((s, hp), BF16), jax.ShapeDtypeStruct((s, hp), BF16), jax.ShapeDtypeStruct((nb, w, t), BF16),
    )
    out_specs = (
        row_spec(w), row_spec(w), vt_spec,
        pl.BlockSpec((N_HEADS, nb, t), lambda i: (0, 0, i)),
        row_spec(w), row_spec(w), vt_spec,
        row_spec(hp), row_spec(hp), vt_spec,
    )
    in_specs = [
        pl.BlockSpec((t, d), lambda i: (i, 0)),
        _const_spec((3, d)), _const_spec((1, d)),
        _const_spec(wp["w_row"].shape), _const_spec(wp["w_t"].shape), _const_spec(wp["w_uq"].shape),
        _const_spec(wp["w_uk"].shape), _const_spec(wp["w_uvt"].shape),
        _const_spec((1, MLA_Q_LORA)), _const_spec((1, MLA_KV_LORA)),
    ] + [tab_spec] * 6
    return pl.pallas_call(
        functools.partial(_proj_kernel, n_blocks=nb),
        grid=(nb,),
        in_specs=in_specs,
        out_specs=out_specs,
        out_shape=out_shape,
        scratch_shapes=[pltpu.VMEM((nb, w), F32)],
        compiler_params=_params(1),
        name="mixer_proj",
    )(x, mod, g_pre.reshape(1, d), wp["w_row"], wp["w_t"], wp["w_uq"], wp["w_uk"], wp["w_uvt"],
      wp["q_norm"], wp["kv_norm"], *tabs)


ATT_TILE = 512
SLAB_CHUNKS = ATT_TILE // SEQ_TILE
PAIR = 2
MOBA_GROUP = 8
MLA_GROUP = 4
ONES_ROWS = 16
LOG2E = 1.4426950408889634
SB_EXP_IS_ZERO = -104.0


def _key_rows(c):
    return pl.ds(pl.multiple_of(c * SEQ_TILE, SEQ_TILE), SEQ_TILE)


def _slab_rows(j):
    return pl.ds(pl.multiple_of(j * ATT_TILE, ATT_TILE), ATT_TILE)


def _packed_heads(q_ref, k_ref, group):
    lane = lax.broadcasted_iota(jnp.int32, (1, LANES), 1)
    qs = []
    for e in range(group):
        q = q_ref[:, (e // PAIR) * LANES:(e // PAIR + 1) * LANES]
        qs.append(jnp.where((lane >= HEAD_DIM) if e % PAIR else (lane < HEAD_DIM), q, jnp.zeros_like(q)))
    return qs, lambda j, e: k_ref[_slab_rows(j), (e // PAIR) * LANES:(e // PAIR + 1) * LANES]


def _wide_heads(q_ref, k_ref, group):
    qs = [q_ref[:, e * LANES:(e + 1) * LANES] for e in range(group)]
    return qs, lambda j, e: k_ref[_slab_rows(j), e * LANES:(e + 1) * LANES]


def _value_t(vt_ref, c, e):
    return vt_ref[c, e * HEAD_DIM:(e + 1) * HEAD_DIM, :]


def _diag_positions(c):
    key = lax.broadcasted_iota(jnp.int32, (SEQ_TILE, ATT_TILE), 0) + c * SEQ_TILE
    qry = lax.broadcasted_iota(jnp.int32, (SEQ_TILE, ATT_TILE), 1)
    return key, qry


def _softmax_attention(q_pair, vt_ref, o_ref, s_refs, diag_mask, past_offset):
    qs, keys = q_pair
    group = len(qs)
    own = pl.program_id(1)
    ones = jnp.ones((ONES_ROWS, SEQ_TILE), BF16)

    def score(e, j, diagonal):
        col_max, offsets = None, []
        slab = _dot_nt(keys(j, e), qs[e])
        for c in range(SLAB_CHUNKS):
            cg = j * SLAB_CHUNKS + c
            s = slab[c * SEQ_TILE:(c + 1) * SEQ_TILE, :]
            off = None
            if diagonal:
                s = diag_mask(e, c, cg, s)
            elif past_offset is not None:
                off = past_offset(e, cg)
            s_refs[e][c * SEQ_TILE:(c + 1) * SEQ_TILE, :] = s
            cm = jnp.max(s, axis=0, keepdims=True)
            cm = cm if off is None else cm + off
            col_max = cm if col_max is None else jnp.maximum(col_max, cm)
            offsets.append(off)
        return col_max, offsets

    def update(e, j, carry, scored):
        m, acc = carry
        col_max, offsets = scored
        m_new = jnp.maximum(m, col_max)
        acc = jnp.exp2(m - m_new) * acc
        for c in range(SLAB_CHUNKS):
            shift = m_new if offsets[c] is None else m_new - offsets[c]
            p = jnp.exp2(s_refs[e][c * SEQ_TILE:(c + 1) * SEQ_TILE, :] - shift)
            v_ones = jnp.concatenate([_value_t(vt_ref, j * SLAB_CHUNKS + c, e), ones], axis=0)
            acc = acc + _dot(v_ones, p.astype(BF16))
        return m_new, acc

    t = ATT_TILE
    init = (jnp.full((1, t), MASKED, F32), jnp.zeros((HEAD_DIM + ONES_ROWS, t), F32))
    last = jnp.maximum(own - 1, 0)

    def ring(j, carries, scored, diagonal, j_ahead):
        carries = list(carries)
        for e in range(group):
            ahead = score(e + 1, j, diagonal) if e + 1 < group else score(0, j_ahead, False)
            carries[e] = update(e, j, carries[e], scored)
            scored = ahead
        return tuple(carries), scored

    def past(j, state):
        return ring(j, state[0], state[1], False, jnp.minimum(j + 1, last))

    state = ring(own, (init,) * group, score(0, own, True), True, 0)
    pairs = lax.shift_right_logical(own, 1)
    state = lax.fori_loop(0, pairs, lambda n, st: past(2 * n + 1, past(2 * n, st)), state)
    carries, _ = lax.fori_loop(2 * pairs, own, past, state)
    out_t = jnp.concatenate([acc[:HEAD_DIM] / acc[HEAD_DIM:HEAD_DIM + 1] for _, acc in carries], axis=0)
    o_ref[...] = out_t.T.astype(o_ref.dtype)


def _moba_kernel(q_ref, k_ref, vt_ref, sel_ref, o_ref, *s_refs):
    def selected(e, cg):
        return sel_ref[e, pl.ds(cg, 1), :]

    def diag_mask(e, c, cg, s):
        key, qry = _diag_positions(c)
        own = lax.shift_right_logical(qry, MOBA_BLOCK.bit_length() - 1) == c
        return jnp.where(own, jnp.where(key <= qry, s, MASKED), s + selected(e, cg))

    _softmax_attention(_packed_heads(q_ref, k_ref, len(s_refs)), vt_ref, o_ref, s_refs, diag_mask, selected)


def _mla_kernel(q_ref, k_ref, vt_ref, o_ref, *s_refs):
    def diag_mask(e, c, cg, s):
        key, qry = _diag_positions(c)
        return jnp.where(key <= qry, s, MASKED)

    _softmax_attention(_wide_heads(q_ref, k_ref, len(s_refs)), vt_ref, o_ref, s_refs, diag_mask, None)


def _log1m_beta(z):
    return -(jnp.maximum(z, 0.0) + jnp.log1p(jnp.exp(-jnp.abs(z))))


def _sb_kernel(q_ref, k_ref, vt_ref, up_ref, o_ref, a_ref, hi_ref, lo_ref):
    i = pl.program_id(0)
    t = SEQ_TILE
    upper = up_ref[...]
    lane = lax.broadcasted_iota(jnp.int32, (1, LANES), 1)
    qs = []
    for h in range(N_HEADS):
        q = q_ref[:, (h // PAIR) * LANES:(h // PAIR + 1) * LANES]
        qs.append(jnp.where((lane >= HEAD_DIM) if h % PAIR else (lane < HEAD_DIM), q, jnp.zeros_like(q)))

    def score(h, j, keep):
        z = _dot_nt(k_ref[_key_rows(j), (h // PAIR) * LANES:(h // PAIR + 1) * LANES], qs[h])
        log1m = _log1m_beta(z)
        if keep is not None:
            log1m = jnp.where(keep, log1m, 0.0)
        hi = log1m.astype(BF16)
        a_ref[h] = z + log1m
        hi_ref[h] = hi
        lo_ref[h] = (log1m - hi.astype(F32)).astype(BF16)
        return jnp.sum(log1m, axis=0, keepdims=True)

    def weigh(h, j, carry, col_sum, keep):
        c, acc = carry
        suffix = _dot(upper, hi_ref[h]) + _dot(upper, lo_ref[h]) + c
        wgt = jnp.exp(a_ref[h] + suffix)
        if keep is not None:
            wgt = jnp.where(keep, wgt, 0.0)
        return c + col_sum, acc + _dot(_value_t(vt_ref, j, h), wgt.astype(BF16))

    def chunk(j, carry, diagonal):
        keep = None
        if diagonal:
            keep = lax.broadcasted_iota(jnp.int32, (t, t), 0) < lax.broadcasted_iota(jnp.int32, (t, t), 1)
        out = []
        col_sum = score(0, j, keep)
        for h in range(N_HEADS):
            nxt = score(h + 1, j, keep) if h + 1 < N_HEADS else None
            out.append(weigh(h, j, carry[h], col_sum, keep))
            col_sum = nxt
        return tuple(out)

    init = ((jnp.zeros((1, t), F32), jnp.zeros((HEAD_DIM, t), F32)),) * N_HEADS
    carry = chunk(i, init, True)

    def alive(state):
        n, carry = state
        c_max = carry[0][0]
        for h in range(1, N_HEADS):
            c_max = jnp.maximum(c_max, carry[h][0])
        return jnp.logical_and(n < i, jnp.max(c_max) > SB_EXP_IS_ZERO)

    _, carry = lax.while_loop(alive, lambda st: (st[0] + 1, chunk(i - 1 - st[0], st[1], False)), (0, carry))
    o_ref[...] = jnp.concatenate([carry[h][1] for h in range(N_HEADS)], axis=0).T.astype(o_ref.dtype)


def _stick_breaking(q, k, vt, upper):
    s, w = q.shape
    t = SEQ_TILE
    return pl.pallas_call(
        _sb_kernel,
        grid=(s // t,),
        in_specs=[pl.BlockSpec((t, w), lambda i: (i, 0)), _const_spec((s, w)), _const_spec((s // t, w, t)),
                  _const_spec((t, t))],
        out_specs=pl.BlockSpec((t, w), lambda i: (i, 0)),
        out_shape=jax.ShapeDtypeStruct((s, w), BF16),
        scratch_shapes=[pltpu.VMEM((N_HEADS, t, t), F32), pltpu.VMEM((N_HEADS, t, t), BF16),
                        pltpu.VMEM((N_HEADS, t, t), BF16)],
        compiler_params=_params(1),
        name="stick_breaking_attention",
    )(q, k, vt, upper)


def _attention(kernel, q, k, vt, extra, extra_specs, group, group_lanes, name):
    s = q.shape[0]
    nb = s // SEQ_TILE
    rows = group * HEAD_DIM
    return pl.pallas_call(
        kernel,
        grid=(N_HEADS // group, s // ATT_TILE),
        in_specs=[
            pl.BlockSpec((ATT_TILE, group_lanes), lambda g, i: (i, g)),
            pl.BlockSpec((s, group_lanes), lambda g, i: (0, g), pipeline_mode=pl.Buffered(1)),
            pl.BlockSpec((nb, rows, SEQ_TILE), lambda g, i: (0, g, 0), pipeline_mode=pl.Buffered(1)),
        ] + extra_specs,
        out_specs=pl.BlockSpec((ATT_TILE, rows), lambda g, i: (i, g)),
        out_shape=jax.ShapeDtypeStruct((s, N_HEADS * HEAD_DIM), BF16),
        scratch_shapes=[pltpu.VMEM((ATT_TILE, ATT_TILE), F32)] * group,
        compiler_params=_params(2),
        name=name,
    )(q, k, vt, *extra)


def _merge_kernel(x_ref, ya_ref, yb_ref, yc_ref, wg_ref, bg_ref, wb_ref, wo_ref, mod_ref, gpre_ref, gpost_ref, o_ref):
    d = D_MODEL
    x = x_ref[...]
    shift, scale, gate = mod_ref[0:1, :], mod_ref[1:2, :], mod_ref[2:3, :]
    hn = (_rms_norm(x, gpre_ref[...]) * (1.0 + scale) + shift).astype(BF16)
    g = jax.nn.sigmoid(_dot(hn, wg_ref[...]) + bg_ref[...])
    merged = None
    for n, y_ref in enumerate((ya_ref, yb_ref, yc_ref)):
        part = _dot(y_ref[...], wb_ref[n]) * g[:, n * d:(n + 1) * d]
        merged = part if merged is None else merged + part
    out = _dot(merged.astype(BF16), wo_ref[...])
    o_ref[...] = x + (1.0 + gate) * _rms_norm(out, gpost_ref[...])


def _merge(x, ya, yb, yc, wp, mod, g_pre, g_post):
    s, d = x.shape
    t = ATT_TILE
    row_spec = lambda n: pl.BlockSpec((t, n), lambda i: (i, 0))
    return pl.pallas_call(
        _merge_kernel,
        grid=(s // t,),
        in_specs=[
            row_spec(d), row_spec(BRANCH_W), row_spec(BRANCH_W), row_spec(BRANCH_W),
            _const_spec(wp["w_g"].shape), _const_spec((1, N_BRANCH * d)), _const_spec(wp["w_b"].shape),
            _const_spec(wp["w_o"].shape), _const_spec((3, d)), _const_spec((1, d)), _const_spec((1, d)),
        ],
        out_specs=row_spec(d),
        out_shape=jax.ShapeDtypeStruct((s, d), F32),
        compiler_params=_params(1),
        name="mixer_merge",
    )(x, ya, yb, yc, wp["w_g"], wp["b_gate"], wp["w_b"], wp["w_o"], mod, g_pre.reshape(1, d), g_post.reshape(1, d))


def _rope_tables(n_rot, seq, first_lane, period):
    half = n_rot // 2
    inv = jnp.power(jnp.float32(ROPE_THETA), -jnp.arange(0, n_rot, 2, dtype=F32) / n_rot)
    ang = jnp.arange(seq, dtype=F32)[:, None] * inv[None, :]
    cos, sin = jnp.cos(ang), jnp.sin(ang)
    lane = jnp.arange(LANES) % period - first_lane
    lo = (lane >= 0) & (lane < half)
    hi = (lane >= half) & (lane < n_rot)
    idx = jnp.clip(lane, 0, n_rot - 1) % half
    a = jnp.where((lo | hi)[None, :], cos[:, idx], 1.0)
    b = jnp.where(lo[None, :], -sin[:, idx], 0.0)
    c = jnp.where(hi[None, :], sin[:, idx], 0.0)
    return a, b, c


def _pad_heads(w, n_heads, width):
    k = w.shape[0]
    w = w.reshape(k, n_heads, width)
    return jnp.pad(w, ((0, 0), (0, 0), (0, HEAD_PAD - width))).reshape(k, n_heads * HEAD_PAD)


def _mixer_weights(w_in, b_gate, q_norm, w_uq, kv_norm, w_ukv, w_branch, w_out):
    w = BRANCH_W
    d = D_MODEL
    qa, ka, va, qb, kb, vb = (w_in[:, n * w:(n + 1) * w] for n in range(6))
    o = 6 * w
    w_cq = w_in[:, o:o + MLA_Q_LORA]
    o += MLA_Q_LORA
    w_ckv = w_in[:, o:o + MLA_KV_LORA]
    o += MLA_KV_LORA
    w_kr = jnp.pad(w_in[:, o:o + MLA_ROPE], ((0, 0), (MLA_NOPE, HEAD_PAD - MLA_QK)))
    o += MLA_ROPE
    w_gates = w_in[:, o:]
    ukv = w_ukv.reshape(MLA_KV_LORA, N_HEADS, MLA_NOPE + MLA_V)
    return {
        "w_row": jnp.concatenate([qa, ka, qb, kb, w_cq, w_ckv, w_kr], axis=1).astype(BF16),
        "w_g": w_gates.astype(BF16),
        "w_t": jnp.concatenate([va, vb], axis=1).T.astype(BF16),
        "w_uq": _pad_heads(w_uq, N_HEADS, MLA_QK).astype(BF16),
        "w_uk": _pad_heads(ukv[:, :, :MLA_NOPE].reshape(MLA_KV_LORA, -1), N_HEADS, MLA_NOPE).astype(BF16),
        "w_uvt": ukv[:, :, MLA_NOPE:].reshape(MLA_KV_LORA, -1).T.astype(BF16),
        "q_norm": q_norm.reshape(1, -1),
        "kv_norm": kv_norm.reshape(1, -1),
        "b_gate": b_gate.reshape(1, N_BRANCH * d),
        "w_b": w_branch.astype(BF16),
        "w_o": w_out.astype(BF16),
    }


def kernel(x, c, ada_w, ada_b, norm_pre, norm_post, ffn_w_gate, ffn_w_up, ffn_w_down, mix_w_in, mix_b_gate,
           mla_q_norm, mla_w_uq, mla_kv_norm, mla_w_ukv, mix_w_branch, mix_w_out):
    bsz, s, d = x.shape
    assert bsz == 1 and d == D_MODEL and s % ATT_TILE == 0 and s % min(FFN_TILE, s) == 0
    depth = ada_w.shape[0]
    t = SEQ_TILE
    mod = _modulation(c, ada_w, ada_b)
    tabs = _rope_tables(PARTIAL_ROT, s, 0, HEAD_DIM) + _rope_tables(MLA_ROPE, s, MLA_NOPE, HEAD_PAD)
    upper = (jnp.arange(t)[None, :] > jnp.arange(t)[:, None]).astype(BF16)
    xs = x.reshape(s, d)
    for l in range(depth):
        xs = _ffn(xs, mod[l, 0], norm_pre[l, 0], norm_post[l, 0], ffn_w_gate[l, 0].astype(BF16),
                  ffn_w_up[l, 0].astype(BF16), ffn_w_down[l, 0].astype(BF16), 0.5)
        wp = _mixer_weights(mix_w_in[l], mix_b_gate[l], mla_q_norm[l], mla_w_uq[l], mla_kv_norm[l], mla_w_ukv[l],
                            mix_w_branch[l], mix_w_out[l])
        qa, ka, vat, sel, qb, kb, vbt, qc, kc, vct = _mixer_proj(xs, mod[l, 1], norm_pre[l, 1], wp, tabs)
        ya = _attention(_moba_kernel, qa, ka, vat, (sel,),
                        [pl.BlockSpec((MOBA_GROUP, s // t, ATT_TILE), lambda g, i: (g, 0, i))], MOBA_GROUP,
                        MOBA_GROUP // PAIR * LANES, "moba_attention")
        yb = _stick_breaking(qb, kb, vbt, upper)
        yc = _attention(_mla_kernel, qc, kc, vct, (), [], MLA_GROUP, MLA_GROUP * HEAD_PAD, "mla_attention")
        xs = _merge(xs, ya, yb, yc, wp, mod[l, 1], norm_pre[l, 1], norm_post[l, 1])
        xs = _ffn(xs, mod[l, 2], norm_pre[l, 2], norm_post[l, 2], ffn_w_gate[l, 1].astype(BF16),
                  ffn_w_up[l, 1].astype(BF16), ffn_w_down[l, 1].astype(BF16), 0.5)
    return xs.reshape(bsz, s, d)
```

```python
import functools

import jax
import jax.numpy as jnp
from jax import lax
from jax.experimental import pallas as pl
from jax.experimental.pallas import tpu as pltpu

F32 = jnp.float32
BF16 = jnp.bfloat16

D_MODEL = 1024
N_HEADS = 8
HEAD_DIM = 64
MOBA_BLOCK = 256
MOBA_TOPK = 3
MLA_Q_LORA = 256
MLA_KV_LORA = 128
MLA_NOPE = 64
MLA_ROPE = 32
MLA_V = 64
MLA_QK = MLA_NOPE + MLA_ROPE
ROPE_THETA = 500000.0
PARTIAL_ROT = HEAD_DIM // 4
D_FF = 2816
N_BRANCH = 3
BRANCH_W = N_HEADS * HEAD_DIM
N_SUB = 3
EPS = 1e-6

LANES = 128
V7X_VMEM_LIMIT = 56 * 1024 * 1024

SEQ_TILE = MOBA_BLOCK
FFN_TILE = 512
MASKED = -1e30
HEAD_PAD = LANES


def _dot(a, b):
    return jnp.dot(a, b, preferred_element_type=F32)


def _dot_nt(a, b, precision=None):
    return lax.dot_general(a, b, (((1,), (1,)), ((), ())), precision=precision, preferred_element_type=F32)


def _rms_norm(x, g):
    return x * lax.rsqrt(jnp.mean(x * x, axis=-1, keepdims=True) + EPS) * g


def _const_spec(shape):
    nd = len(shape)
    return pl.BlockSpec(shape, lambda *_: (0,) * nd, pipeline_mode=pl.Buffered(1))


def _params(n_grid):
    return pltpu.CompilerParams(dimension_semantics=("arbitrary",) * n_grid, vmem_limit_bytes=V7X_VMEM_LIMIT)


def _mod_kernel(c_ref, w_ref, b_ref, o_ref):
    c = c_ref[...]
    ca = c * jax.nn.sigmoid(c)
    o_ref[0] = jnp.dot(ca, w_ref[0], precision=lax.Precision.HIGHEST, preferred_element_type=F32) + b_ref[0]


def _modulation(c, ada_w, ada_b):
    depth, d, n = ada_w.shape
    tn = n // 8
    c8 = jnp.broadcast_to(c.reshape(1, d), (8, d))
    out = pl.pallas_call(
        _mod_kernel,
        grid=(depth, n // tn),
        in_specs=[
            pl.BlockSpec((8, d), lambda l, j: (0, 0)),
            pl.BlockSpec((1, d, tn), lambda l, j: (l, 0, j)),
            pl.BlockSpec((1, 1, tn), lambda l, j: (l, 0, j)),
        ],
        out_specs=pl.BlockSpec((1, 8, tn), lambda l, j: (l, 0, j)),
        out_shape=jax.ShapeDtypeStruct((depth, 8, n), F32),
        compiler_params=_params(2),
        name="adaln_mod",
    )(c8, ada_w, ada_b.reshape(depth, 1, n))
    return out[:, 0, :].reshape(depth, N_SUB, 3, d)


def _ffn_kernel(x_ref, mod_ref, gpre_ref, gpost_ref, wg_ref, wu_ref, wd_ref, o_ref, *, res_w):
    x = x_ref[...]
    shift, scale, gate = mod_ref[0:1, :], mod_ref[1:2, :], mod_ref[2:3, :]
    h = (_rms_norm(x, gpre_ref[...]) * (1.0 + scale) + shift).astype(BF16)
    g = _dot(h, wg_ref[...])
    u = _dot(h, wu_ref[...])
    a = (g * jax.nn.sigmoid(g) * u).astype(BF16)
    y = _dot(a, wd_ref[...])
    o_ref[...] = x + (res_w * (1.0 + gate)) * _rms_norm(y, gpost_ref[...])


def _ffn(x, mod, g_pre, g_post, wg, wu, wd, layer, half, res_w):
    s, d = x.shape
    tm = min(FFN_TILE, s)
    dff = wg.shape[-1]
    weight_spec = lambda shape: pl.BlockSpec((None, None) + shape, lambda i: (layer, half, 0, 0),
                                             pipeline_mode=pl.Buffered(1))
    return pl.pallas_call(
        functools.partial(_ffn_kernel, res_w=res_w),
        grid=(s // tm,),
        in_specs=[
            pl.BlockSpec((tm, d), lambda i: (i, 0)),
            _const_spec((3, d)),
            _const_spec((1, d)),
            _const_spec((1, d)),
            weight_spec((d, dff)),
            weight_spec((d, dff)),
            weight_spec((dff, d)),
        ],
        out_specs=pl.BlockSpec((tm, d), lambda i: (i, 0)),
        out_shape=jax.ShapeDtypeStruct((s, d), F32),
        compiler_params=_params(1),
        name="macaron_ffn",
    )(x, mod, g_pre.reshape(1, d), g_post.reshape(1, d), wg, wu, wd)


def _rope_rows(x, a, b, c, half):
    outs = []
    for g in range(x.shape[1] // LANES):
        xg = x[:, g * LANES:(g + 1) * LANES]
        outs.append(xg * a + pltpu.roll(xg, LANES - half, 1) * b + pltpu.roll(xg, half, 1) * c)
    return jnp.concatenate(outs, axis=1) if len(outs) > 1 else outs[0]


def _proj_kernel(x_ref, mod_ref, gpre_ref, wrow_ref, wt_ref, wuq_ref, wuk_ref, wuvt_ref, qn_ref, kvn_ref,
                 pa_ref, pb_ref, pc_ref, ma_ref, mb_ref, mc_ref,
                 qa_ref, ka_ref, vat_ref, sel_ref, qb_ref, kb_ref, vbt_ref, qc_ref, kc_ref, vct_ref,
                 kmean_ref, *, n_blocks):
    i = pl.program_id(0)
    w = BRANCH_W
    x = x_ref[...]
    shift, scale = mod_ref[0:1, :], mod_ref[1:2, :]
    hn = (_rms_norm(x, gpre_ref[...]) * (1.0 + scale) + shift).astype(BF16)
    rows = _dot(hn, wrow_ref[...])
    cols_t = _dot_nt(wt_ref[...], hn)

    pa, pb, pc = pa_ref[...], pb_ref[...], pc_ref[...]
    q_a = _rope_rows(rows[:, 0:w], pa, pb, pc, PARTIAL_ROT // 2)
    k_a = _rope_rows(rows[:, w:2 * w], pa, pb, pc, PARTIAL_ROT // 2)
    qa_ref[...] = (q_a * (HEAD_DIM ** -0.5 * LOG2E)).astype(BF16)
    ka_ref[...] = k_a.astype(BF16)
    vat_ref[0] = cols_t[0:w, :].astype(BF16)

    @pl.when(i == 0)
    def _():
        kmean_ref[...] = jnp.zeros_like(kmean_ref)

    kmean_ref[pl.ds(i, 1), :] = jnp.mean(k_a, axis=0, keepdims=True)
    kmean = kmean_ref[...]
    t = x.shape[0]
    lane = lax.broadcasted_iota(jnp.int32, (1, LANES), 1)
    blk = lax.broadcasted_iota(jnp.int32, (n_blocks, t), 0)
    past = blk < i
    blk_f = blk.astype(F32)
    for h in range(N_HEADS):
        p, e = h // 2, h % 2
        in_head = (lane >= HEAD_DIM) if e else (lane < HEAD_DIM)
        qm = jnp.where(in_head, q_a[:, p * LANES:(p + 1) * LANES], 0.0)
        gate = _dot_nt(kmean[:, p * LANES:(p + 1) * LANES], qm, precision=lax.Precision.HIGHEST)
        gate = jnp.where(past, gate, -jnp.inf)
        chosen = jnp.zeros(gate.shape, F32)
        for _ in range(min(MOBA_TOPK, n_blocks)):
            top = jnp.max(gate, axis=0, keepdims=True)
            first = jnp.min(jnp.where(gate == top, blk_f, float(n_blocks)), axis=0, keepdims=True)
            pick = blk_f == first
            chosen = jnp.where(pick, 1.0, chosen)
            gate = jnp.where(pick, -jnp.inf, gate)
        sel_ref[h] = jnp.where(jnp.logical_and(chosen > 0.0, past), 0.0, MASKED)

    qb_ref[...] = (rows[:, 2 * w:3 * w] * (HEAD_DIM ** -0.5 * LOG2E)).astype(BF16)
    kb_ref[...] = rows[:, 3 * w:4 * w].astype(BF16)
    vbt_ref[0] = cols_t[w:2 * w, :].astype(BF16)

    o = 4 * w
    ma, mb, mc = ma_ref[...], mb_ref[...], mc_ref[...]
    cq = _rms_norm(rows[:, o:o + MLA_Q_LORA], qn_ref[...]).astype(BF16)
    q_c = _rope_rows(_dot(cq, wuq_ref[...]), ma, mb, mc, MLA_ROPE // 2)
    qc_ref[...] = (q_c * (MLA_QK ** -0.5 * LOG2E)).astype(BF16)
    o += MLA_Q_LORA
    ckv = _rms_norm(rows[:, o:o + MLA_KV_LORA], kvn_ref[...]).astype(BF16)
    o += MLA_KV_LORA
    k_rope = _rope_rows(rows[:, o:o + HEAD_PAD], ma, mb, mc, MLA_ROPE // 2)
    kc_ref[...] = (_dot(ckv, wuk_ref[...]) + jnp.concatenate([k_rope] * N_HEADS, axis=1)).astype(BF16)
    vct_ref[0] = _dot_nt(wuvt_ref[...], ckv).astype(BF16)


def _mixer_proj(x, mod, g_pre, wp, tabs):
    s, d = x.shape
    t = SEQ_TILE
    nb = s // t
    w = BRANCH_W
    hp = N_HEADS * HEAD_PAD
    row_spec = lambda n: pl.BlockSpec((t, n), lambda i: (i, 0))
    vt_spec = pl.BlockSpec((1, w, t), lambda i: (i, 0, 0))
    tab_spec = pl.BlockSpec((t, LANES), lambda i: (i, 0))
    out_shape = (
        jax.ShapeDtypeStruct((s, w), BF16), jax.ShapeDtypeStruct((s, w), BF16), jax.ShapeDtypeStruct((nb, w, t), BF16),
        jax.ShapeDtypeStruct((N_HEADS, nb, s), F32),
        jax.ShapeDtypeStruct((s, w), BF16), jax.ShapeDtypeStruct((s, w), BF16), jax.ShapeDtypeStruct((nb, w, t), BF16),
        jax.ShapeDtypeStruct((s, hp), BF16), jax.ShapeDtypeStruct((s, hp), BF16), jax.ShapeDtypeStruct((nb, w, t), BF16),
    )
    out_specs = (
        row_spec(w), row_spec(w), vt_spec,
        pl.BlockSpec((N_HEADS, nb, t), lambda i: (0, 0, i)),
        row_spec(w), row_spec(w), vt_spec,
        row_spec(hp), row_spec(hp), vt_spec,
    )
    in_specs = [
        pl.BlockSpec((t, d), lambda i: (i, 0)),
        _const_spec((3, d)), _const_spec((1, d)),
        _const_spec(wp["w_row"].shape), _const_spec(wp["w_t"].shape), _const_spec(wp["w_uq"].shape),
        _const_spec(wp["w_uk"].shape), _const_spec(wp["w_uvt"].shape),
        _const_spec((1, MLA_Q_LORA)), _const_spec((1, MLA_KV_LORA)),
    ] + [tab_spec] * 6
    return pl.pallas_call(
        functools.partial(_proj_kernel, n_blocks=nb),
        grid=(nb,),
        in_specs=in_specs,
        out_specs=out_specs,
        out_shape=out_shape,
        scratch_shapes=[pltpu.VMEM((nb, w), F32)],
        compiler_params=_params(1),
        name="mixer_proj",
    )(x, mod, g_pre.reshape(1, d), wp["w_row"], wp["w_t"], wp["w_uq"], wp["w_uk"], wp["w_uvt"],
      wp["q_norm"], wp["kv_norm"], *tabs)


ATT_TILE = 512
SLAB_CHUNKS = ATT_TILE // SEQ_TILE
PAIR = 2
MOBA_GROUP = 8
MLA_GROUP = 4
ONES_ROWS = 16
LOG2E = 1.4426950408889634
SB_EXP2_IS_ZERO = -150.0


def _key_rows(c):
    return pl.ds(pl.multiple_of(c * SEQ_TILE, SEQ_TILE), SEQ_TILE)


def _slab_rows(j):
    return pl.ds(pl.multiple_of(j * ATT_TILE, ATT_TILE), ATT_TILE)


def _packed_heads(q_ref, k_ref, group):
    lane = lax.broadcasted_iota(jnp.int32, (1, LANES), 1)
    qs = []
    for e in range(group):
        q = q_ref[:, (e // PAIR) * LANES:(e // PAIR + 1) * LANES]
        qs.append(jnp.where((lane >= HEAD_DIM) if e % PAIR else (lane < HEAD_DIM), q, jnp.zeros_like(q)))
    return qs, lambda j, e: k_ref[_slab_rows(j), (e // PAIR) * LANES:(e // PAIR + 1) * LANES]


def _wide_heads(q_ref, k_ref, group):
    qs = [q_ref[:, e * LANES:(e + 1) * LANES] for e in range(group)]
    return qs, lambda j, e: k_ref[_slab_rows(j), e * LANES:(e + 1) * LANES]


def _value_t(vt_ref, c, e):
    return vt_ref[c, e * HEAD_DIM:(e + 1) * HEAD_DIM, :]


def _diag_positions(c):
    key = lax.broadcasted_iota(jnp.int32, (SEQ_TILE, ATT_TILE), 0) + c * SEQ_TILE
    qry = lax.broadcasted_iota(jnp.int32, (SEQ_TILE, ATT_TILE), 1)
    return key, qry


def _softmax_attention(q_pair, vt_ref, o_ref, s_refs, diag_mask, past_offset):
    qs, keys = q_pair
    group = len(qs)
    own = pl.program_id(1)
    ones = jnp.ones((ONES_ROWS, SEQ_TILE), BF16)

    def score(e, j, diagonal):
        col_max, offsets = None, []
        slab = _dot_nt(keys(j, e), qs[e])
        for c in range(SLAB_CHUNKS):
            cg = j * SLAB_CHUNKS + c
            s = slab[c * SEQ_TILE:(c + 1) * SEQ_TILE, :]
            off = None
            if diagonal:
                s = diag_mask(e, c, cg, s)
            elif past_offset is not None:
                off = past_offset(e, cg)
            s_refs[e][c * SEQ_TILE:(c + 1) * SEQ_TILE, :] = s
            cm = jnp.max(s, axis=0, keepdims=True)
            cm = cm if off is None else cm + off
            col_max = cm if col_max is None else jnp.maximum(col_max, cm)
            offsets.append(off)
        return col_max, offsets

    def update(e, j, carry, scored):
        m, acc = carry
        col_max, offsets = scored
        m_new = jnp.maximum(m, col_max)
        acc = jnp.exp2(m - m_new) * acc
        for c in range(SLAB_CHUNKS):
            shift = m_new if offsets[c] is None else m_new - offsets[c]
            p = jnp.exp2(s_refs[e][c * SEQ_TILE:(c + 1) * SEQ_TILE, :] - shift)
            v_ones = jnp.concatenate([_value_t(vt_ref, j * SLAB_CHUNKS + c, e), ones], axis=0)
            acc = acc + _dot(v_ones, p.astype(BF16))
        return m_new, acc

    t = ATT_TILE
    init = (jnp.full((1, t), MASKED, F32), jnp.zeros((HEAD_DIM + ONES_ROWS, t), F32))
    last = jnp.maximum(own - 1, 0)

    def ring(j, carries, scored, diagonal, j_ahead):
        carries = list(carries)
        for e in range(group):
            ahead = score(e + 1, j, diagonal) if e + 1 < group else score(0, j_ahead, False)
            carries[e] = update(e, j, carries[e], scored)
            scored = ahead
        return tuple(carries), scored

    def past(j, state):
        return ring(j, state[0], state[1], False, jnp.minimum(j + 1, last))

    state = ring(own, (init,) * group, score(0, own, True), True, 0)
    quads = lax.shift_right_logical(own, 2)

    def four(n, st):
        for u in range(4):
            st = past(4 * n + u, st)
        return st

    state = lax.fori_loop(0, quads, four, state)
    carries, _ = lax.fori_loop(4 * quads, own, past, state)
    out_t = jnp.concatenate([acc[:HEAD_DIM] / acc[HEAD_DIM:HEAD_DIM + 1] for _, acc in carries], axis=0)
    o_ref[...] = out_t.T.astype(o_ref.dtype)


def _moba_kernel(q_ref, k_ref, vt_ref, sel_ref, o_ref, *s_refs):
    def selected(e, cg):
        return sel_ref[e, pl.ds(cg, 1), :]

    def diag_mask(e, c, cg, s):
        key, qry = _diag_positions(c)
        own = lax.shift_right_logical(qry, MOBA_BLOCK.bit_length() - 1) == c
        return jnp.where(own, jnp.where(key <= qry, s, MASKED), s + selected(e, cg))

    _softmax_attention(_packed_heads(q_ref, k_ref, len(s_refs)), vt_ref, o_ref, s_refs, diag_mask, selected)


def _mla_kernel(q_ref, k_ref, vt_ref, o_ref, *s_refs):
    def diag_mask(e, c, cg, s):
        key, qry = _diag_positions(c)
        return jnp.where(key <= qry, s, MASKED)

    _softmax_attention(_wide_heads(q_ref, k_ref, len(s_refs)), vt_ref, o_ref, s_refs, diag_mask, None)


def _log2_1m_beta(z2):
    return -(jnp.maximum(z2, 0.0) + jnp.log2(1.0 + jnp.exp2(-jnp.abs(z2))))


def _sb_kernel(q_ref, k_ref, vt_ref, up_ref, o_ref, a_ref, hi_ref, lo_ref):
    i = pl.program_id(0)
    t = SEQ_TILE
    upper = up_ref[...]
    lane = lax.broadcasted_iota(jnp.int32, (1, LANES), 1)
    qs = []
    for h in range(N_HEADS):
        q = q_ref[:, (h // PAIR) * LANES:(h // PAIR + 1) * LANES]
        qs.append(jnp.where((lane >= HEAD_DIM) if h % PAIR else (lane < HEAD_DIM), q, jnp.zeros_like(q)))

    def score(h, j, keep):
        z = _dot_nt(k_ref[_key_rows(j), (h // PAIR) * LANES:(h // PAIR + 1) * LANES], qs[h])
        log1m = _log2_1m_beta(z)
        if keep is not None:
            log1m = jnp.where(keep, log1m, 0.0)
        hi = log1m.astype(BF16)
        a_ref[h] = z + log1m
        hi_ref[h] = hi
        lo_ref[h] = (log1m - hi.astype(F32)).astype(BF16)
        return jnp.sum(log1m, axis=0, keepdims=True)

    def weigh(h, j, carry, col_sum, keep):
        c, acc = carry
        suffix = _dot(upper, hi_ref[h]) + _dot(upper, lo_ref[h]) + c
        wgt = jnp.exp2(a_ref[h] + suffix)
        if keep is not None:
            wgt = jnp.where(keep, wgt, 0.0)
        return c + col_sum, acc + _dot(_value_t(vt_ref, j, h), wgt.astype(BF16))

    def chunk(j, carry, diagonal):
        keep = None
        if diagonal:
            keep = lax.broadcasted_iota(jnp.int32, (t, t), 0) < lax.broadcasted_iota(jnp.int32, (t, t), 1)
        out = []
        col_sum = score(0, j, keep)
        for h in range(N_HEADS):
            nxt = score(h + 1, j, keep) if h + 1 < N_HEADS else None
            out.append(weigh(h, j, carry[h], col_sum, keep))
            col_sum = nxt
        return tuple(out)

    init = ((jnp.zeros((1, t), F32), jnp.zeros((HEAD_DIM, t), F32)),) * N_HEADS
    carry = chunk(i, init, True)

    def alive(state):
        n, carry = state
        c_max = carry[0][0]
        for h in range(1, N_HEADS):
            c_max = jnp.maximum(c_max, carry[h][0])
        return jnp.logical_and(n < i, jnp.max(c_max) > SB_EXP2_IS_ZERO)

    _, carry = lax.while_loop(alive, lambda st: (st[0] + 1, chunk(i - 1 - st[0], st[1], False)), (0, carry))
    o_ref[...] = jnp.concatenate([carry[h][1] for h in range(N_HEADS)], axis=0).T.astype(o_ref.dtype)


def _stick_breaking(q, k, vt, upper):
    s, w = q.shape
    t = SEQ_TILE
    return pl.pallas_call(
        _sb_kernel,
        grid=(s // t,),
        in_specs=[pl.BlockSpec((t, w), lambda i: (i, 0)), _const_spec((s, w)), _const_spec((s // t, w, t)),
                  _const_spec((t, t))],
        out_specs=pl.BlockSpec((t, w), lambda i: (i, 0)),
        out_shape=jax.ShapeDtypeStruct((s, w), BF16),
        scratch_shapes=[pltpu.VMEM((N_HEADS, t, t), F32), pltpu.VMEM((N_HEADS, t, t), BF16),
                        pltpu.VMEM((N_HEADS, t, t), BF16)],
        compiler_params=_params(1),
        name="stick_breaking_attention",
    )(q, k, vt, upper)


def _attention(kernel, q, k, vt, extra, extra_specs, group, group_lanes, name):
    s = q.shape[0]
    nb = s // SEQ_TILE
    rows = group * HEAD_DIM
    return pl.pallas_call(
        kernel,
        grid=(N_HEADS // group, s // ATT_TILE),
        in_specs=[
            pl.BlockSpec((ATT_TILE, group_lanes), lambda g, i: (i, g)),
            pl.BlockSpec((s, group_lanes), lambda g, i: (0, g), pipeline_mode=pl.Buffered(1)),
            pl.BlockSpec((nb, rows, SEQ_TILE), lambda g, i: (0, g, 0), pipeline_mode=pl.Buffered(1)),
        ] + extra_specs,
        out_specs=pl.BlockSpec((ATT_TILE, rows), lambda g, i: (i, g)),
        out_shape=jax.ShapeDtypeStruct((s, N_HEADS * HEAD_DIM), BF16),
        scratch_shapes=[pltpu.VMEM((ATT_TILE, ATT_TILE), F32)] * group,
        compiler_params=_params(2),
        name=name,
    )(q, k, vt, *extra)


def _merge_kernel(x_ref, ya_ref, yb_ref, yc_ref, wg_ref, bg_ref, wb_ref, wo_ref, mod_ref, gpre_ref, gpost_ref, o_ref):
    d = D_MODEL
    x = x_ref[...]
    shift, scale, gate = mod_ref[0:1, :], mod_ref[1:2, :], mod_ref[2:3, :]
    hn = (_rms_norm(x, gpre_ref[...]) * (1.0 + scale) + shift).astype(BF16)
    g = jax.nn.sigmoid(_dot(hn, wg_ref[...]) + bg_ref[...])
    merged = None
    for n, y_ref in enumerate((ya_ref, yb_ref, yc_ref)):
        part = _dot(y_ref[...], wb_ref[n]) * g[:, n * d:(n + 1) * d]
        merged = part if merged is None else merged + part
    out = _dot(merged.astype(BF16), wo_ref[...])
    o_ref[...] = x + (1.0 + gate) * _rms_norm(out, gpost_ref[...])


def _merge(x, ya, yb, yc, wp, mod, g_pre, g_post):
    s, d = x.shape
    t = ATT_TILE
    row_spec = lambda n: pl.BlockSpec((t, n), lambda i: (i, 0))
    return pl.pallas_call(
        _merge_kernel,
        grid=(s // t,),
        in_specs=[
            row_spec(d), row_spec(BRANCH_W), row_spec(BRANCH_W), row_spec(BRANCH_W),
            _const_spec(wp["w_g"].shape), _const_spec((1, N_BRANCH * d)), _const_spec(wp["w_b"].shape),
            _const_spec(wp["w_o"].shape), _const_spec((3, d)), _const_spec((1, d)), _const_spec((1, d)),
        ],
        out_specs=row_spec(d),
        out_shape=jax.ShapeDtypeStruct((s, d), F32),
        compiler_params=_params(1),
        name="mixer_merge",
    )(x, ya, yb, yc, wp["w_g"], wp["b_gate"], wp["w_b"], wp["w_o"], mod, g_pre.reshape(1, d), g_post.reshape(1, d))


def _rope_tables(n_rot, seq, first_lane, period):
    half = n_rot // 2
    inv = jnp.power(jnp.float32(ROPE_THETA), -jnp.arange(0, n_rot, 2, dtype=F32) / n_rot)
    ang = jnp.arange(seq, dtype=F32)[:, None] * inv[None, :]
    cos, sin = jnp.cos(ang), jnp.sin(ang)
    lane = jnp.arange(LANES) % period - first_lane
    lo = (lane >= 0) & (lane < half)
    hi = (lane >= half) & (lane < n_rot)
    idx = jnp.clip(lane, 0, n_rot - 1) % half
    a = jnp.where((lo | hi)[None, :], cos[:, idx], 1.0)
    b = jnp.where(lo[None, :], -sin[:, idx], 0.0)
    c = jnp.where(hi[None, :], sin[:, idx], 0.0)
    return a, b, c


def _pad_heads(w, n_heads, width):
    k = w.shape[0]
    w = w.reshape(k, n_heads, width)
    return jnp.pad(w, ((0, 0), (0, 0), (0, HEAD_PAD - width))).reshape(k, n_heads * HEAD_PAD)


def _mixer_weights(w_in, b_gate, q_norm, w_uq, kv_norm, w_ukv, w_branch, w_out):
    w = BRANCH_W
    d = D_MODEL
    qa, ka, va, qb, kb, vb = (w_in[:, n * w:(n + 1) * w] for n in range(6))
    o = 6 * w
    w_cq = w_in[:, o:o + MLA_Q_LORA]
    o += MLA_Q_LORA
    w_ckv = w_in[:, o:o + MLA_KV_LORA]
    o += MLA_KV_LORA
    w_kr = jnp.pad(w_in[:, o:o + MLA_ROPE], ((0, 0), (MLA_NOPE, HEAD_PAD - MLA_QK)))
    o += MLA_ROPE
    w_gates = w_in[:, o:]
    ukv = w_ukv.reshape(MLA_KV_LORA, N_HEADS, MLA_NOPE + MLA_V)
    return {
        "w_row": jnp.concatenate([qa, ka, qb, kb, w_cq, w_ckv, w_kr], axis=1),
        "w_g": w_gates,
        "w_t": jnp.concatenate([va, vb], axis=1).T,
        "w_uq": _pad_heads(w_uq, N_HEADS, MLA_QK).astype(BF16),
        "w_uk": _pad_heads(ukv[:, :, :MLA_NOPE].reshape(MLA_KV_LORA, -1), N_HEADS, MLA_NOPE).astype(BF16),
        "w_uvt": ukv[:, :, MLA_NOPE:].reshape(MLA_KV_LORA, -1).T.astype(BF16),
        "q_norm": q_norm.reshape(1, -1),
        "kv_norm": kv_norm.reshape(1, -1),
        "b_gate": b_gate.reshape(1, N_BRANCH * d),
        "w_b": w_branch,
        "w_o": w_out,
    }


def kernel(x, c, ada_w, ada_b, norm_pre, norm_post, ffn_w_gate, ffn_w_up, ffn_w_down, mix_w_in, mix_b_gate,
           mla_q_norm, mla_w_uq, mla_kv_norm, mla_w_ukv, mix_w_branch, mix_w_out):
    bsz, s, d = x.shape
    assert bsz == 1 and d == D_MODEL and s % ATT_TILE == 0 and s % min(FFN_TILE, s) == 0
    depth = ada_w.shape[0]
    t = SEQ_TILE
    mod = _modulation(c, ada_w, ada_b)
    tabs = _rope_tables(PARTIAL_ROT, s, 0, HEAD_DIM) + _rope_tables(MLA_ROPE, s, MLA_NOPE, HEAD_PAD)
    upper = (jnp.arange(t)[None, :] > jnp.arange(t)[:, None]).astype(BF16)
    xs = x.reshape(s, d)
    wg, wu, wd = ffn_w_gate.astype(BF16), ffn_w_up.astype(BF16), ffn_w_down.astype(BF16)
    w_in, w_branch, w_out = mix_w_in.astype(BF16), mix_w_branch.astype(BF16), mix_w_out.astype(BF16)
    for l in range(depth):
        xs = _ffn(xs, mod[l, 0], norm_pre[l, 0], norm_post[l, 0], wg, wu, wd, l, 0, 0.5)
        wp = _mixer_weights(w_in[l], mix_b_gate[l], mla_q_norm[l], mla_w_uq[l], mla_kv_norm[l], mla_w_ukv[l],
                            w_branch[l], w_out[l])
        qa, ka, vat, sel, qb, kb, vbt, qc, kc, vct = _mixer_proj(xs, mod[l, 1], norm_pre[l, 1], wp, tabs)
        ya = _attention(_moba_kernel, qa, ka, vat, (sel,),
                        [pl.BlockSpec((MOBA_GROUP, s // t, ATT_TILE), lambda g, i: (g, 0, i))], MOBA_GROUP,
                        MOBA_GROUP // PAIR * LANES, "moba_attention")
        yb = _stick_breaking(qb, kb, vbt, upper)
        yc = _attention(_mla_kernel, qc, kc, vct, (), [], MLA_GROUP, MLA_GROUP * HEAD_PAD, "mla_attention")
        xs = _merge(xs, ya, yb, yc, wp, mod[l, 1], norm_pre[l, 1], norm_post[l, 1])
        xs = _ffn(xs, mod[l, 2], norm_pre[l, 2], norm_post[l, 2], wg, wu, wd, l, 1, 0.5)
    return xs.reshape(bsz, s, d)
```

```python
import functools

import jax
import jax.numpy as jnp
from jax import lax
from jax.experimental import pallas as pl
from jax.experimental.pallas import tpu as pltpu

F32 = jnp.float32
BF16 = jnp.bfloat16

D_MODEL = 1024
N_HEADS = 8
HEAD_DIM = 64
MOBA_BLOCK = 256
MOBA_TOPK = 3
MLA_Q_LORA = 256
MLA_KV_LORA = 128
MLA_NOPE = 64
MLA_ROPE = 32
MLA_V = 64
MLA_QK = MLA_NOPE + MLA_ROPE
ROPE_THETA = 500000.0
PARTIAL_ROT = HEAD_DIM // 4
D_FF = 2816
N_BRANCH = 3
BRANCH_W = N_HEADS * HEAD_DIM
N_SUB = 3
EPS = 1e-6

LANES = 128
V7X_VMEM_LIMIT = 56 * 1024 * 1024

SEQ_TILE = MOBA_BLOCK
FFN_TILE = 512
MASKED = -1e30
HEAD_PAD = LANES


def _dot(a, b):
    return jnp.dot(a, b, preferred_element_type=F32)


def _dot_nt(a, b, precision=None):
    return lax.dot_general(a, b, (((1,), (1,)), ((), ())), precision=precision, preferred_element_type=F32)


def _rms_norm(x, g):
    return x * lax.rsqrt(jnp.mean(x * x, axis=-1, keepdims=True) + EPS) * g


def _const_spec(shape):
    nd = len(shape)
    return pl.BlockSpec(shape, lambda *_: (0,) * nd, pipeline_mode=pl.Buffered(1))


def _params(n_grid):
    return pltpu.CompilerParams(dimension_semantics=("arbitrary",) * n_grid, vmem_limit_bytes=V7X_VMEM_LIMIT)


def _mod_kernel(c_ref, w_ref, b_ref, o_ref):
    c = c_ref[...]
    ca = c * jax.nn.sigmoid(c)
    o_ref[0] = jnp.dot(ca, w_ref[0], precision=lax.Precision.HIGHEST, preferred_element_type=F32) + b_ref[0]


def _modulation(c, ada_w, ada_b):
    depth, d, n = ada_w.shape
    tn = n // 8
    c8 = jnp.broadcast_to(c.reshape(1, d), (8, d))
    out = pl.pallas_call(
        _mod_kernel,
        grid=(depth, n // tn),
        in_specs=[
            pl.BlockSpec((8, d), lambda l, j: (0, 0)),
            pl.BlockSpec((1, d, tn), lambda l, j: (l, 0, j)),
            pl.BlockSpec((1, 1, tn), lambda l, j: (l, 0, j)),
        ],
        out_specs=pl.BlockSpec((1, 8, tn), lambda l, j: (l, 0, j)),
        out_shape=jax.ShapeDtypeStruct((depth, 8, n), F32),
        compiler_params=_params(2),
        name="adaln_mod",
    )(c8, ada_w, ada_b.reshape(depth, 1, n))
    return out[:, 0, :].reshape(depth, N_SUB, 3, d)


def _ffn_kernel(x_ref, mod_ref, gpre_ref, gpost_ref, wg_ref, wu_ref, wd_ref, o_ref, *, res_w):
    x = x_ref[...]
    shift, scale, gate = mod_ref[0:1, :], mod_ref[1:2, :], mod_ref[2:3, :]
    h = (_rms_norm(x, gpre_ref[...]) * (1.0 + scale) + shift).astype(BF16)
    g = _dot(h, wg_ref[...])
    u = _dot(h, wu_ref[...])
    a = (g * jax.nn.sigmoid(g) * u).astype(BF16)
    y = _dot(a, wd_ref[...])
    o_ref[...] = x + (res_w * (1.0 + gate)) * _rms_norm(y, gpost_ref[...])


def _ffn(x, mod, g_pre, g_post, wg, wu, wd, layer, half, res_w):
    s, d = x.shape
    tm = min(FFN_TILE, s)
    dff = wg.shape[-1]
    weight_spec = lambda shape: pl.BlockSpec((None, None) + shape, lambda i: (layer, half, 0, 0),
                                             pipeline_mode=pl.Buffered(1))
    return pl.pallas_call(
        functools.partial(_ffn_kernel, res_w=res_w),
        grid=(s // tm,),
        in_specs=[
            pl.BlockSpec((tm, d), lambda i: (i, 0)),
            _const_spec((3, d)),
            _const_spec((1, d)),
            _const_spec((1, d)),
            weight_spec((d, dff)),
            weight_spec((d, dff)),
            weight_spec((dff, d)),
        ],
        out_specs=pl.BlockSpec((tm, d), lambda i: (i, 0)),
        out_shape=jax.ShapeDtypeStruct((s, d), F32),
        compiler_params=_params(1),
        name="macaron_ffn",
    )(x, mod, g_pre.reshape(1, d), g_post.reshape(1, d), wg, wu, wd)


def _rope_rows(x, a, b, c, half):
    outs = []
    for g in range(x.shape[1] // LANES):
        xg = x[:, g * LANES:(g + 1) * LANES]
        outs.append(xg * a + pltpu.roll(xg, LANES - half, 1) * b + pltpu.roll(xg, half, 1) * c)
    return jnp.concatenate(outs, axis=1) if len(outs) > 1 else outs[0]


def _proj_kernel(x_ref, mod_ref, gpre_ref, wrow_ref, wt_ref, wuq_ref, wuk_ref, wuvt_ref, qn_ref, kvn_ref,
                 pa_ref, pb_ref, pc_ref, ma_ref, mb_ref, mc_ref,
                 qa_ref, ka_ref, vat_ref, sel_ref, qb_ref, kb_ref, vbt_ref, qc_ref, kc_ref, vct_ref,
                 kmean_ref, *, n_blocks):
    i = pl.program_id(0)
    w = BRANCH_W
    x = x_ref[...]
    shift, scale = mod_ref[0:1, :], mod_ref[1:2, :]
    hn = (_rms_norm(x, gpre_ref[...]) * (1.0 + scale) + shift).astype(BF16)
    rows = _dot(hn, wrow_ref[...])
    cols_t = _dot_nt(wt_ref[...], hn)

    pa, pb, pc = pa_ref[...], pb_ref[...], pc_ref[...]
    q_a = _rope_rows(rows[:, 0:w], pa, pb, pc, PARTIAL_ROT // 2)
    k_a = _rope_rows(rows[:, w:2 * w], pa, pb, pc, PARTIAL_ROT // 2)
    qa_ref[...] = (q_a * (HEAD_DIM ** -0.5 * LOG2E)).astype(BF16)
    ka_ref[...] = k_a.astype(BF16)
    vat_ref[0] = cols_t[0:w, :].astype(BF16)

    @pl.when(i == 0)
    def _():
        kmean_ref[...] = jnp.zeros_like(kmean_ref)

    kmean_ref[pl.ds(i, 1), :] = jnp.mean(k_a, axis=0, keepdims=True)
    kmean = kmean_ref[...]
    t = x.shape[0]
    lane = lax.broadcasted_iota(jnp.int32, (1, LANES), 1)
    blk = lax.broadcasted_iota(jnp.int32, (n_blocks, t), 0)
    past = blk < i
    blk_f = blk.astype(F32)
    for h in range(N_HEADS):
        p, e = h // 2, h % 2
        in_head = (lane >= HEAD_DIM) if e else (lane < HEAD_DIM)
        qm = jnp.where(in_head, q_a[:, p * LANES:(p + 1) * LANES], 0.0)
        gate = _dot_nt(kmean[:, p * LANES:(p + 1) * LANES], qm, precision=lax.Precision.HIGHEST)
        gate = jnp.where(past, gate, -jnp.inf)
        chosen = jnp.zeros(gate.shape, F32)
        for _ in range(min(MOBA_TOPK, n_blocks)):
            top = jnp.max(gate, axis=0, keepdims=True)
            first = jnp.min(jnp.where(gate == top, blk_f, float(n_blocks)), axis=0, keepdims=True)
            pick = blk_f == first
            chosen = jnp.where(pick, 1.0, chosen)
            gate = jnp.where(pick, -jnp.inf, gate)
        sel_ref[h] = jnp.where(jnp.logical_and(chosen > 0.0, past), 0.0, MASKED)

    qb_ref[...] = (rows[:, 2 * w:3 * w] * (HEAD_DIM ** -0.5 * LOG2E)).astype(BF16)
    kb_ref[...] = rows[:, 3 * w:4 * w].astype(BF16)
    vbt_ref[0] = cols_t[w:2 * w, :].astype(BF16)

    o = 4 * w
    ma, mb, mc = ma_ref[...], mb_ref[...], mc_ref[...]
    cq = _rms_norm(rows[:, o:o + MLA_Q_LORA], qn_ref[...]).astype(BF16)
    q_c = _rope_rows(_dot(cq, wuq_ref[...]), ma, mb, mc, MLA_ROPE // 2)
    qc_ref[...] = (q_c * (MLA_QK ** -0.5 * LOG2E)).astype(BF16)
    o += MLA_Q_LORA
    ckv = _rms_norm(rows[:, o:o + MLA_KV_LORA], kvn_ref[...]).astype(BF16)
    o += MLA_KV_LORA
    k_rope = _rope_rows(rows[:, o:o + HEAD_PAD], ma, mb, mc, MLA_ROPE // 2)
    kc_ref[...] = (_dot(ckv, wuk_ref[...]) + jnp.concatenate([k_rope] * N_HEADS, axis=1)).astype(BF16)
    vct_ref[0] = _dot_nt(wuvt_ref[...], ckv).astype(BF16)


def _mixer_proj(x, mod, g_pre, wp, tabs):
    s, d = x.shape
    t = SEQ_TILE
    nb = s // t
    w = BRANCH_W
    hp = N_HEADS * HEAD_PAD
    row_spec = lambda n: pl.BlockSpec((t, n), lambda i: (i, 0))
    vt_spec = pl.BlockSpec((1, w, t), lambda i: (i, 0, 0))
    tab_spec = pl.BlockSpec((t, LANES), lambda i: (i, 0))
    out_shape = (
        jax.ShapeDtypeStruct((s, w), BF16), jax.ShapeDtypeStruct((s, w), BF16), jax.ShapeDtypeStruct((nb, w, t), BF16),
        jax.ShapeDtypeStruct((N_HEADS, nb, s), F32),
        jax.ShapeDtypeStruct((s, w), BF16), jax.ShapeDtypeStruct((s, w), BF16), jax.ShapeDtypeStruct((nb, w, t), BF16),
        jax.ShapeDtypeStruct((s, hp), BF16), jax.ShapeDtypeStruct((s, hp), BF16), jax.ShapeDtypeStruct((nb, w, t), BF16),
    )
    out_specs = (
        row_spec(w), row_spec(w), vt_spec,
        pl.BlockSpec((N_HEADS, nb, t), lambda i: (0, 0, i)),
        row_spec(w), row_spec(w), vt_spec,
        row_spec(hp), row_spec(hp), vt_spec,
    )
    in_specs = [
        pl.BlockSpec((t, d), lambda i: (i, 0)),
        _const_spec((3, d)), _const_spec((1, d)),
        _const_spec(wp["w_row"].shape), _const_spec(wp["w_t"].shape), _const_spec(wp["w_uq"].shape),
        _const_spec(wp["w_uk"].shape), _const_spec(wp["w_uvt"].shape),
        _const_spec((1, MLA_Q_LORA)), _const_spec((1, MLA_KV_LORA)),
    ] + [tab_spec] * 6
    return pl.pallas_call(
        functools.partial(_proj_kernel, n_blocks=nb),
        grid=(nb,),
        in_specs=in_specs,
        out_specs=out_specs,
        out_shape=out_shape,
        scratch_shapes=[pltpu.VMEM((nb, w), F32)],
        compiler_params=_params(1),
        name="mixer_proj",
    )(x, mod, g_pre.reshape(1, d), wp["w_row"], wp["w_t"], wp["w_uq"], wp["w_uk"], wp["w_uvt"],
      wp["q_norm"], wp["kv_norm"], *tabs)


ATT_TILE = 512
SLAB_CHUNKS = ATT_TILE // SEQ_TILE
PAIR = 2
MOBA_GROUP = 8
MLA_GROUP = 4
ONES_ROWS = 16
LOG2E = 1.4426950408889634


def _key_rows(c):
    return pl.ds(pl.multiple_of(c * SEQ_TILE, SEQ_TILE), SEQ_TILE)


def _slab_rows(j):
    return pl.ds(pl.multiple_of(j * ATT_TILE, ATT_TILE), ATT_TILE)


def _packed_heads(q_ref, k_ref, group):
    lane = lax.broadcasted_iota(jnp.int32, (1, LANES), 1)
    qs = []
    for e in range(group):
        q = q_ref[:, (e // PAIR) * LANES:(e // PAIR + 1) * LANES]
        qs.append(jnp.where((lane >= HEAD_DIM) if e % PAIR else (lane < HEAD_DIM), q, jnp.zeros_like(q)))
    return qs, lambda j, e: k_ref[_slab_rows(j), (e // PAIR) * LANES:(e // PAIR + 1) * LANES]


def _wide_heads(q_ref, k_ref, group):
    qs = [q_ref[:, e * LANES:(e + 1) * LANES] for e in range(group)]
    return qs, lambda j, e: k_ref[_slab_rows(j), e * LANES:(e + 1) * LANES]


def _value_t(vt_ref, c, e):
    return vt_ref[c, e * HEAD_DIM:(e + 1) * HEAD_DIM, :]


def _diag_positions(c):
    key = lax.broadcasted_iota(jnp.int32, (SEQ_TILE, ATT_TILE), 0) + c * SEQ_TILE
    qry = lax.broadcasted_iota(jnp.int32, (SEQ_TILE, ATT_TILE), 1)
    return key, qry


def _softmax_attention(q_pair, vt_ref, o_ref, s_refs, diag_mask, past_offset):
    qs, keys = q_pair
    group = len(qs)
    own = pl.program_id(1)
    ones = jnp.ones((ONES_ROWS, SEQ_TILE), BF16)

    def score(e, j, diagonal):
        col_max, offsets = None, []
        slab = _dot_nt(keys(j, e), qs[e])
        for c in range(SLAB_CHUNKS):
            cg = j * SLAB_CHUNKS + c
            s = slab[c * SEQ_TILE:(c + 1) * SEQ_TILE, :]
            off = None
            if diagonal:
                s = diag_mask(e, c, cg, s)
            elif past_offset is not None:
                off = past_offset(e, cg)
            s_refs[e][c * SEQ_TILE:(c + 1) * SEQ_TILE, :] = s
            cm = jnp.max(s, axis=0, keepdims=True)
            cm = cm if off is None else cm + off
            col_max = cm if col_max is None else jnp.maximum(col_max, cm)
            offsets.append(off)
        return col_max, offsets

    def update(e, j, carry, scored):
        m, acc = carry
        col_max, offsets = scored
        m_new = jnp.maximum(m, col_max)
        acc = jnp.exp2(m - m_new) * acc
        for c in range(SLAB_CHUNKS):
            shift = m_new if offsets[c] is None else m_new - offsets[c]
            p = jnp.exp2(s_refs[e][c * SEQ_TILE:(c + 1) * SEQ_TILE, :] - shift)
            v_ones = jnp.concatenate([_value_t(vt_ref, j * SLAB_CHUNKS + c, e), ones], axis=0)
            acc = acc + _dot(v_ones, p.astype(BF16))
        return m_new, acc

    t = ATT_TILE
    init = (jnp.full((1, t), MASKED, F32), jnp.zeros((HEAD_DIM + ONES_ROWS, t), F32))
    last = jnp.maximum(own - 1, 0)

    def ring(j, carries, scored, diagonal, j_ahead):
        carries = list(carries)
        for e in range(group):
            ahead = score(e + 1, j, diagonal) if e + 1 < group else score(0, j_ahead, False)
            carries[e] = update(e, j, carries[e], scored)
            scored = ahead
        return tuple(carries), scored

    def past(j, state):
        return ring(j, state[0], state[1], False, jnp.minimum(j + 1, last))

    state = ring(own, (init,) * group, score(0, own, True), True, 0)
    quads = lax.shift_right_logical(own, 2)

    def four(n, st):
        for u in range(4):
            st = past(4 * n + u, st)
        return st

    state = lax.fori_loop(0, quads, four, state)
    carries, _ = lax.fori_loop(4 * quads, own, past, state)
    out_t = jnp.concatenate([acc[:HEAD_DIM] / acc[HEAD_DIM:HEAD_DIM + 1] for _, acc in carries], axis=0)
    o_ref[...] = out_t.T.astype(o_ref.dtype)


def _moba_kernel(q_ref, k_ref, vt_ref, sel_ref, o_ref, *s_refs):
    def selected(e, cg):
        return sel_ref[e, pl.ds(cg, 1), :]

    def diag_mask(e, c, cg, s):
        key, qry = _diag_positions(c)
        own = lax.shift_right_logical(qry, MOBA_BLOCK.bit_length() - 1) == c
        return jnp.where(own, jnp.where(key <= qry, s, MASKED), s + selected(e, cg))

    _softmax_attention(_packed_heads(q_ref, k_ref, len(s_refs)), vt_ref, o_ref, s_refs, diag_mask, selected)


def _mla_kernel(q_ref, k_ref, vt_ref, o_ref, *s_refs):
    def diag_mask(e, c, cg, s):
        key, qry = _diag_positions(c)
        return jnp.where(key <= qry, s, MASKED)

    _softmax_attention(_wide_heads(q_ref, k_ref, len(s_refs)), vt_ref, o_ref, s_refs, diag_mask, None)


def _log2_1m_beta(z2):
    return -(jnp.maximum(z2, 0.0) + jnp.log2(1.0 + jnp.exp2(-jnp.abs(z2))))


def _sb_kernel(q_ref, k_ref, vt_ref, up_ref, o_ref, a_ref, hi_ref, lo_ref):
    i = pl.program_id(0)
    t = SEQ_TILE
    upper = up_ref[...]
    lane = lax.broadcasted_iota(jnp.int32, (1, LANES), 1)
    qs = []
    for h in range(N_HEADS):
        q = q_ref[:, (h // PAIR) * LANES:(h // PAIR + 1) * LANES]
        qs.append(jnp.where((lane >= HEAD_DIM) if h % PAIR else (lane < HEAD_DIM), q, jnp.zeros_like(q)))

    def score(h, j, keep):
        z = _dot_nt(k_ref[_key_rows(j), (h // PAIR) * LANES:(h // PAIR + 1) * LANES], qs[h])
        log1m = _log2_1m_beta(z)
        if keep is not None:
            log1m = jnp.where(keep, log1m, 0.0)
        hi = log1m.astype(BF16)
        a_ref[h] = z + log1m
        hi_ref[h] = hi
        lo_ref[h] = (log1m - hi.astype(F32)).astype(BF16)
        return jnp.sum(log1m, axis=0, keepdims=True)

    def weigh(h, j, carry, col_sum, keep):
        c, acc = carry
        suffix = _dot(upper, hi_ref[h]) + _dot(upper, lo_ref[h]) + c
        wgt = jnp.exp2(a_ref[h] + suffix)
        if keep is not None:
            wgt = jnp.where(keep, wgt, 0.0)
        return c + col_sum, acc + _dot(_value_t(vt_ref, j, h), wgt.astype(BF16))

    def chunk(j, carry, diagonal):
        keep = None
        if diagonal:
            keep = lax.broadcasted_iota(jnp.int32, (t, t), 0) < lax.broadcasted_iota(jnp.int32, (t, t), 1)
        out = []
        col_sum = score(0, j, keep)
        for h in range(N_HEADS):
            nxt = score(h + 1, j, keep) if h + 1 < N_HEADS else None
            out.append(weigh(h, j, carry[h], col_sum, keep))
            col_sum = nxt
        return tuple(out)

    init = ((jnp.zeros((1, t), F32), jnp.zeros((HEAD_DIM, t), F32)),) * N_HEADS
    carry = chunk(i, init, True)

    carry = lax.fori_loop(0, i, lambda n, carry: chunk(i - 1 - n, carry, False), carry)
    o_ref[...] = jnp.concatenate([carry[h][1] for h in range(N_HEADS)], axis=0).T.astype(o_ref.dtype)


def _stick_breaking(q, k, vt, upper):
    s, w = q.shape
    t = SEQ_TILE
    return pl.pallas_call(
        _sb_kernel,
        grid=(s // t,),
        in_specs=[pl.BlockSpec((t, w), lambda i: (i, 0)), _const_spec((s, w)), _const_spec((s // t, w, t)),
                  _const_spec((t, t))],
        out_specs=pl.BlockSpec((t, w), lambda i: (i, 0)),
        out_shape=jax.ShapeDtypeStruct((s, w), BF16),
        scratch_shapes=[pltpu.VMEM((N_HEADS, t, t), F32), pltpu.VMEM((N_HEADS, t, t), BF16),
                        pltpu.VMEM((N_HEADS, t, t), BF16)],
        compiler_params=_params(1),
        name="stick_breaking_attention",
    )(q, k, vt, upper)


def _attention(kernel, q, k, vt, extra, extra_specs, group, group_lanes, name):
    s = q.shape[0]
    nb = s // SEQ_TILE
    rows = group * HEAD_DIM
    return pl.pallas_call(
        kernel,
        grid=(N_HEADS // group, s // ATT_TILE),
        in_specs=[
            pl.BlockSpec((ATT_TILE, group_lanes), lambda g, i: (i, g)),
            pl.BlockSpec((s, group_lanes), lambda g, i: (0, g), pipeline_mode=pl.Buffered(1)),
            pl.BlockSpec((nb, rows, SEQ_TILE), lambda g, i: (0, g, 0), pipeline_mode=pl.Buffered(1)),
        ] + extra_specs,
        out_specs=pl.BlockSpec((ATT_TILE, rows), lambda g, i: (i, g)),
        out_shape=jax.ShapeDtypeStruct((s, N_HEADS * HEAD_DIM), BF16),
        scratch_shapes=[pltpu.VMEM((ATT_TILE, ATT_TILE), F32)] * group,
        compiler_params=_params(2),
        name=name,
    )(q, k, vt, *extra)


def _merge_kernel(x_ref, ya_ref, yb_ref, yc_ref, wg_ref, bg_ref, wb_ref, wo_ref, mod_ref, gpre_ref, gpost_ref, o_ref):
    d = D_MODEL
    x = x_ref[...]
    shift, scale, gate = mod_ref[0:1, :], mod_ref[1:2, :], mod_ref[2:3, :]
    hn = (_rms_norm(x, gpre_ref[...]) * (1.0 + scale) + shift).astype(BF16)
    g = jax.nn.sigmoid(_dot(hn, wg_ref[...]) + bg_ref[...])
    merged = None
    for n, y_ref in enumerate((ya_ref, yb_ref, yc_ref)):
        part = _dot(y_ref[...], wb_ref[n]) * g[:, n * d:(n + 1) * d]
        merged = part if merged is None else merged + part
    out = _dot(merged.astype(BF16), wo_ref[...])
    o_ref[...] = x + (1.0 + gate) * _rms_norm(out, gpost_ref[...])


def _merge(x, ya, yb, yc, wp, mod, g_pre, g_post):
    s, d = x.shape
    t = ATT_TILE
    row_spec = lambda n: pl.BlockSpec((t, n), lambda i: (i, 0))
    return pl.pallas_call(
        _merge_kernel,
        grid=(s // t,),
        in_specs=[
            row_spec(d), row_spec(BRANCH_W), row_spec(BRANCH_W), row_spec(BRANCH_W),
            _const_spec(wp["w_g"].shape), _const_spec((1, N_BRANCH * d)), _const_spec(wp["w_b"].shape),
            _const_spec(wp["w_o"].shape), _const_spec((3, d)), _const_spec((1, d)), _const_spec((1, d)),
        ],
        out_specs=row_spec(d),
        out_shape=jax.ShapeDtypeStruct((s, d), F32),
        compiler_params=_params(1),
        name="mixer_merge",
    )(x, ya, yb, yc, wp["w_g"], wp["b_gate"], wp["w_b"], wp["w_o"], mod, g_pre.reshape(1, d), g_post.reshape(1, d))


def _rope_tables(n_rot, seq, first_lane, period):
    half = n_rot // 2
    inv = jnp.power(jnp.float32(ROPE_THETA), -jnp.arange(0, n_rot, 2, dtype=F32) / n_rot)
    ang = jnp.arange(seq, dtype=F32)[:, None] * inv[None, :]
    cos, sin = jnp.cos(ang), jnp.sin(ang)
    lane = jnp.arange(LANES) % period - first_lane
    lo = (lane >= 0) & (lane < half)
    hi = (lane >= half) & (lane < n_rot)
    idx = jnp.clip(lane, 0, n_rot - 1) % half
    a = jnp.where((lo | hi)[None, :], cos[:, idx], 1.0)
    b = jnp.where(lo[None, :], -sin[:, idx], 0.0)
    c = jnp.where(hi[None, :], sin[:, idx], 0.0)
    return a, b, c


def _pad_heads(w, n_heads, width):
    k = w.shape[0]
    w = w.reshape(k, n_heads, width)
    return jnp.pad(w, ((0, 0), (0, 0), (0, HEAD_PAD - width))).reshape(k, n_heads * HEAD_PAD)


def _mixer_weights(w_in, b_gate, q_norm, w_uq, kv_norm, w_ukv, w_branch, w_out):
    w = BRANCH_W
    d = D_MODEL
    qa, ka, va, qb, kb, vb = (w_in[:, n * w:(n + 1) * w] for n in range(6))
    o = 6 * w
    w_cq = w_in[:, o:o + MLA_Q_LORA]
    o += MLA_Q_LORA
    w_ckv = w_in[:, o:o + MLA_KV_LORA]
    o += MLA_KV_LORA
    w_kr = jnp.pad(w_in[:, o:o + MLA_ROPE], ((0, 0), (MLA_NOPE, HEAD_PAD - MLA_QK)))
    o += MLA_ROPE
    w_gates = w_in[:, o:]
    ukv = w_ukv.reshape(MLA_KV_LORA, N_HEADS, MLA_NOPE + MLA_V)
    return {
        "w_row": jnp.concatenate([qa, ka, qb, kb, w_cq, w_ckv, w_kr], axis=1),
        "w_g": w_gates,
        "w_t": jnp.concatenate([va, vb], axis=1).T,
        "w_uq": _pad_heads(w_uq, N_HEADS, MLA_QK).astype(BF16),
        "w_uk": _pad_heads(ukv[:, :, :MLA_NOPE].reshape(MLA_KV_LORA, -1), N_HEADS, MLA_NOPE).astype(BF16),
        "w_uvt": ukv[:, :, MLA_NOPE:].reshape(MLA_KV_LORA, -1).T.astype(BF16),
        "q_norm": q_norm.reshape(1, -1),
        "kv_norm": kv_norm.reshape(1, -1),
        "b_gate": b_gate.reshape(1, N_BRANCH * d),
        "w_b": w_branch,
        "w_o": w_out,
    }


def kernel(x, c, ada_w, ada_b, norm_pre, norm_post, ffn_w_gate, ffn_w_up, ffn_w_down, mix_w_in, mix_b_gate,
           mla_q_norm, mla_w_uq, mla_kv_norm, mla_w_ukv, mix_w_branch, mix_w_out):
    bsz, s, d = x.shape
    assert bsz == 1 and d == D_MODEL and s % ATT_TILE == 0 and s % min(FFN_TILE, s) == 0
    depth = ada_w.shape[0]
    t = SEQ_TILE
    mod = _modulation(c, ada_w, ada_b)
    tabs = _rope_tables(PARTIAL_ROT, s, 0, HEAD_DIM) + _rope_tables(MLA_ROPE, s, MLA_NOPE, HEAD_PAD)
    upper = (jnp.arange(t)[None, :] > jnp.arange(t)[:, None]).astype(BF16)
    xs = x.reshape(s, d)
    wg, wu, wd = ffn_w_gate.astype(BF16), ffn_w_up.astype(BF16), ffn_w_down.astype(BF16)
    w_in, w_branch, w_out = mix_w_in.astype(BF16), mix_w_branch.astype(BF16), mix_w_out.astype(BF16)
    for l in range(depth):
        xs = _ffn(xs, mod[l, 0], norm_pre[l, 0], norm_post[l, 0], wg, wu, wd, l, 0, 0.5)
        wp = _mixer_weights(w_in[l], mix_b_gate[l], mla_q_norm[l], mla_w_uq[l], mla_kv_norm[l], mla_w_ukv[l],
                            w_branch[l], w_out[l])
        qa, ka, vat, sel, qb, kb, vbt, qc, kc, vct = _mixer_proj(xs, mod[l, 1], norm_pre[l, 1], wp, tabs)
        ya = _attention(_moba_kernel, qa, ka, vat, (sel,),
                        [pl.BlockSpec((MOBA_GROUP, s // t, ATT_TILE), lambda g, i: (g, 0, i))], MOBA_GROUP,
                        MOBA_GROUP // PAIR * LANES, "moba_attention")
        yb = _stick_breaking(qb, kb, vbt, upper)
        yc = _attention(_mla_kernel, qc, kc, vct, (), [], MLA_GROUP, MLA_GROUP * HEAD_PAD, "mla_attention")
        xs = _merge(xs, ya, yb, yc, wp, mod[l, 1], norm_pre[l, 1], norm_post[l, 1])
        xs = _ffn(xs, mod[l, 2], norm_pre[l, 2], norm_post[l, 2], wg, wu, wd, l, 1, 0.5)
    return xs.reshape(bsz, s, d)
```

```python
import functools

import jax
import jax.numpy as jnp
from jax import lax
from jax.experimental import pallas as pl
from jax.experimental.pallas import tpu as pltpu

F32 = jnp.float32
BF16 = jnp.bfloat16

D_MODEL = 1024
N_HEADS = 8
HEAD_DIM = 64
MOBA_BLOCK = 256
MOBA_TOPK = 3
MLA_Q_LORA = 256
MLA_KV_LORA = 128
MLA_NOPE = 64
MLA_ROPE = 32
MLA_V = 64
MLA_QK = MLA_NOPE + MLA_ROPE
ROPE_THETA = 500000.0
PARTIAL_ROT = HEAD_DIM // 4
D_FF = 2816
N_BRANCH = 3
BRANCH_W = N_HEADS * HEAD_DIM
N_SUB = 3
EPS = 1e-6

LANES = 128
V7X_VMEM_LIMIT = 56 * 1024 * 1024

SEQ_TILE = MOBA_BLOCK
FFN_TILE = 512
MASKED = -1e30
HEAD_PAD = LANES


def _dot(a, b):
    return jnp.dot(a, b, preferred_element_type=F32)


def _dot_nt(a, b, precision=None):
    return lax.dot_general(a, b, (((1,), (1,)), ((), ())), precision=precision, preferred_element_type=F32)


def _rms_norm(x, g):
    return x * lax.rsqrt(jnp.mean(x * x, axis=-1, keepdims=True) + EPS) * g


def _const_spec(shape):
    nd = len(shape)
    return pl.BlockSpec(shape, lambda *_: (0,) * nd, pipeline_mode=pl.Buffered(1))


def _params(n_grid):
    return pltpu.CompilerParams(dimension_semantics=("arbitrary",) * n_grid, vmem_limit_bytes=V7X_VMEM_LIMIT)


def _mod_kernel(c_ref, w_ref, b_ref, o_ref):
    c = c_ref[...]
    ca = c * jax.nn.sigmoid(c)
    o_ref[0] = jnp.sum(ca * w_ref[0], axis=0, keepdims=True) + b_ref[0]


def _modulation(c, ada_w, ada_b):
    depth, d, n = ada_w.shape
    tn = n // 8
    out = pl.pallas_call(
        _mod_kernel,
        grid=(depth, n // tn),
        in_specs=[
            pl.BlockSpec((d, 1), lambda l, j: (0, 0)),
            pl.BlockSpec((1, d, tn), lambda l, j: (l, 0, j)),
            pl.BlockSpec((1, 1, tn), lambda l, j: (l, 0, j)),
        ],
        out_specs=pl.BlockSpec((1, 1, tn), lambda l, j: (l, 0, j)),
        out_shape=jax.ShapeDtypeStruct((depth, 1, n), F32),
        compiler_params=_params(2),
        name="adaln_mod",
    )(c.reshape(d, 1), ada_w, ada_b.reshape(depth, 1, n))
    return out.reshape(depth, N_SUB, 3, d)


def _ffn_kernel(x_ref, mod_ref, gpre_ref, gpost_ref, wg_ref, wu_ref, wd_ref, o_ref, *, res_w):
    x = x_ref[...]
    shift, scale, gate = mod_ref[0:1, :], mod_ref[1:2, :], mod_ref[2:3, :]
    h = (_rms_norm(x, gpre_ref[...]) * (1.0 + scale) + shift).astype(BF16)
    g = _dot(h, wg_ref[...])
    u = _dot(h, wu_ref[...])
    a = (g * jax.nn.sigmoid(g) * u).astype(BF16)
    y = _dot(a, wd_ref[...])
    o_ref[...] = x + (res_w * (1.0 + gate)) * _rms_norm(y, gpost_ref[...])


def _ffn(x, mod, g_pre, g_post, wg, wu, wd, layer, half, res_w):
    s, d = x.shape
    tm = min(FFN_TILE, s)
    dff = wg.shape[-1]
    weight_spec = lambda shape: pl.BlockSpec((None, None) + shape, lambda i: (layer, half, 0, 0),
                                             pipeline_mode=pl.Buffered(1))
    return pl.pallas_call(
        functools.partial(_ffn_kernel, res_w=res_w),
        grid=(s // tm,),
        in_specs=[
            pl.BlockSpec((tm, d), lambda i: (i, 0)),
            _const_spec((3, d)),
            _const_spec((1, d)),
            _const_spec((1, d)),
            weight_spec((d, dff)),
            weight_spec((d, dff)),
            weight_spec((dff, d)),
        ],
        out_specs=pl.BlockSpec((tm, d), lambda i: (i, 0)),
        out_shape=jax.ShapeDtypeStruct((s, d), F32),
        compiler_params=_params(1),
        name="macaron_ffn",
    )(x, mod, g_pre.reshape(1, d), g_post.reshape(1, d), wg, wu, wd)


def _rope_rows(x, a, b, c, half):
    outs = []
    for g in range(x.shape[1] // LANES):
        xg = x[:, g * LANES:(g + 1) * LANES]
        outs.append(xg * a + pltpu.roll(xg, LANES - half, 1) * b + pltpu.roll(xg, half, 1) * c)
    return jnp.concatenate(outs, axis=1) if len(outs) > 1 else outs[0]


def _proj_kernel(x_ref, mod_ref, gpre_ref, wrow_ref, wt_ref, wuq_ref, wuk_ref, wuvt_ref, qn_ref, kvn_ref,
                 pa_ref, pb_ref, pc_ref, ma_ref, mb_ref, mc_ref,
                 qa_ref, ka_ref, vat_ref, sel_ref, qb_ref, kb_ref, vbt_ref, qc_ref, kc_ref, vct_ref,
                 kmean_ref, *, n_blocks):
    i = pl.program_id(0)
    w = BRANCH_W
    x = x_ref[...]
    shift, scale = mod_ref[0:1, :], mod_ref[1:2, :]
    hn = (_rms_norm(x, gpre_ref[...]) * (1.0 + scale) + shift).astype(BF16)
    rows = _dot(hn, wrow_ref[...])
    cols_t = _dot_nt(wt_ref[...], hn)

    pa, pb, pc = pa_ref[...], pb_ref[...], pc_ref[...]
    q_a = _rope_rows(rows[:, 0:w], pa, pb, pc, PARTIAL_ROT // 2)
    k_a = _rope_rows(rows[:, w:2 * w], pa, pb, pc, PARTIAL_ROT // 2)
    qa_ref[...] = (q_a * (HEAD_DIM ** -0.5 * LOG2E)).astype(BF16)
    ka_ref[...] = k_a.astype(BF16)
    vat_ref[0] = cols_t[0:w, :].astype(BF16)

    @pl.when(i == 0)
    def _():
        kmean_ref[...] = jnp.zeros_like(kmean_ref)

    kmean_ref[pl.ds(i, 1), :] = jnp.mean(k_a, axis=0, keepdims=True)
    kmean = kmean_ref[...]
    t = x.shape[0]
    lane = lax.broadcasted_iota(jnp.int32, (1, LANES), 1)
    blk = lax.broadcasted_iota(jnp.int32, (n_blocks, t), 0)
    past = blk < i
    blk_f = blk.astype(F32)
    for h in range(N_HEADS):
        p, e = h // 2, h % 2
        in_head = (lane >= HEAD_DIM) if e else (lane < HEAD_DIM)
        qm = jnp.where(in_head, q_a[:, p * LANES:(p + 1) * LANES], 0.0)
        gate = _dot_nt(kmean[:, p * LANES:(p + 1) * LANES], qm, precision=lax.Precision.HIGHEST)
        gate = jnp.where(past, gate, -jnp.inf)
        chosen = jnp.zeros(gate.shape, F32)
        for _ in range(min(MOBA_TOPK, n_blocks)):
            top = jnp.max(gate, axis=0, keepdims=True)
            first = jnp.min(jnp.where(gate == top, blk_f, float(n_blocks)), axis=0, keepdims=True)
            pick = blk_f == first
            chosen = jnp.where(pick, 1.0, chosen)
            gate = jnp.where(pick, -jnp.inf, gate)
        sel_ref[h] = jnp.where(jnp.logical_and(chosen > 0.0, past), 0.0, MASKED)

    qb_ref[...] = (rows[:, 2 * w:3 * w] * (HEAD_DIM ** -0.5 * LOG2E)).astype(BF16)
    kb_ref[...] = rows[:, 3 * w:4 * w].astype(BF16)
    vbt_ref[0] = cols_t[w:2 * w, :].astype(BF16)

    o = 4 * w
    ma, mb, mc = ma_ref[...], mb_ref[...], mc_ref[...]
    cq = _rms_norm(rows[:, o:o + MLA_Q_LORA], qn_ref[...]).astype(BF16)
    q_c = _rope_rows(_dot(cq, wuq_ref[...]), ma, mb, mc, MLA_ROPE // 2)
    qc_ref[...] = (q_c * (MLA_QK ** -0.5 * LOG2E)).astype(BF16)
    o += MLA_Q_LORA
    ckv = _rms_norm(rows[:, o:o + MLA_KV_LORA], kvn_ref[...]).astype(BF16)
    o += MLA_KV_LORA
    k_rope = _rope_rows(rows[:, o:o + HEAD_PAD], ma, mb, mc, MLA_ROPE // 2)
    kc_ref[...] = (_dot(ckv, wuk_ref[...]) + jnp.concatenate([k_rope] * N_HEADS, axis=1)).astype(BF16)
    vct_ref[0] = _dot_nt(wuvt_ref[...], ckv).astype(BF16)


def _mixer_proj(x, mod, g_pre, wp, tabs):
    s, d = x.shape
    t = SEQ_TILE
    nb = s // t
    w = BRANCH_W
    hp = N_HEADS * HEAD_PAD
    row_spec = lambda n: pl.BlockSpec((t, n), lambda i: (i, 0))
    vt_spec = pl.BlockSpec((1, w, t), lambda i: (i, 0, 0))
    tab_spec = pl.BlockSpec((t, LANES), lambda i: (i, 0))
    out_shape = (
        jax.ShapeDtypeStruct((s, w), BF16), jax.ShapeDtypeStruct((s, w), BF16), jax.ShapeDtypeStruct((nb, w, t), BF16),
        jax.ShapeDtypeStruct((N_HEADS, nb, s), F32),
        jax.ShapeDtypeStruct((s, w), BF16), jax.ShapeDtypeStruct((s, w), BF16), jax.ShapeDtypeStruct((nb, w, t), BF16),
        jax.ShapeDtypeStruct((s, hp), BF16), jax.ShapeDtypeStruct((s, hp), BF16), jax.ShapeDtypeStruct((nb, w, t), BF16),
    )
    out_specs = (
        row_spec(w), row_spec(w), vt_spec,
        pl.BlockSpec((N_HEADS, nb, t), lambda i: (0, 0, i)),
        row_spec(w), row_spec(w), vt_spec,
        row_spec(hp), row_spec(hp), vt_spec,
    )
    in_specs = [
        pl.BlockSpec((t, d), lambda i: (i, 0)),
        _const_spec((3, d)), _const_spec((1, d)),
        _const_spec(wp["w_row"].shape), _const_spec(wp["w_t"].shape), _const_spec(wp["w_uq"].shape),
        _const_spec(wp["w_uk"].shape), _const_spec(wp["w_uvt"].shape),
        _const_spec((1, MLA_Q_LORA)), _const_spec((1, MLA_KV_LORA)),
    ] + [tab_spec] * 6
    return pl.pallas_call(
        functools.partial(_proj_kernel, n_blocks=nb),
        grid=(nb,),
        in_specs=in_specs,
        out_specs=out_specs,
        out_shape=out_shape,
        scratch_shapes=[pltpu.VMEM((nb, w), F32)],
        compiler_params=_params(1),
        name="mixer_proj",
    )(x, mod, g_pre.reshape(1, d), wp["w_row"], wp["w_t"], wp["w_uq"], wp["w_uk"], wp["w_uvt"],
      wp["q_norm"], wp["kv_norm"], *tabs)


ATT_TILE = 512
SLAB_CHUNKS = ATT_TILE // SEQ_TILE
PAIR = 2
MOBA_GROUP = 8
MLA_GROUP = 4
ONES_ROWS = 16
LOG2E = 1.4426950408889634
SB_EXP2_IS_ZERO = -150.0


def _key_rows(c):
    return pl.ds(pl.multiple_of(c * SEQ_TILE, SEQ_TILE), SEQ_TILE)


def _slab_rows(j):
    return pl.ds(pl.multiple_of(j * ATT_TILE, ATT_TILE), ATT_TILE)


def _packed_heads(q_ref, k_ref, group):
    lane = lax.broadcasted_iota(jnp.int32, (1, LANES), 1)
    qs = []
    for e in range(group):
        q = q_ref[:, (e // PAIR) * LANES:(e // PAIR + 1) * LANES]
        qs.append(jnp.where((lane >= HEAD_DIM) if e % PAIR else (lane < HEAD_DIM), q, jnp.zeros_like(q)))
    return qs, lambda j, e: k_ref[_slab_rows(j), (e // PAIR) * LANES:(e // PAIR + 1) * LANES]


def _wide_heads(q_ref, k_ref, group):
    qs = [q_ref[:, e * LANES:(e + 1) * LANES] for e in range(group)]
    return qs, lambda j, e: k_ref[_slab_rows(j), e * LANES:(e + 1) * LANES]


def _value_t(vt_ref, c, e):
    return vt_ref[c, e * HEAD_DIM:(e + 1) * HEAD_DIM, :]


def _diag_positions(c):
    key = lax.broadcasted_iota(jnp.int32, (SEQ_TILE, ATT_TILE), 0) + c * SEQ_TILE
    qry = lax.broadcasted_iota(jnp.int32, (SEQ_TILE, ATT_TILE), 1)
    return key, qry


def _softmax_attention(q_pair, vt_ref, o_ref, s_refs, diag_mask, past_offset):
    qs, keys = q_pair
    group = len(qs)
    own = pl.program_id(1)
    ones = jnp.ones((ONES_ROWS, SEQ_TILE), BF16)

    def score(e, j, diagonal):
        col_max, offsets = None, []
        slab = _dot_nt(keys(j, e), qs[e])
        for c in range(SLAB_CHUNKS):
            cg = j * SLAB_CHUNKS + c
            s = slab[c * SEQ_TILE:(c + 1) * SEQ_TILE, :]
            off = None
            if diagonal:
                s = diag_mask(e, c, cg, s)
            elif past_offset is not None:
                off = past_offset(e, cg)
            s_refs[e][c * SEQ_TILE:(c + 1) * SEQ_TILE, :] = s
            cm = jnp.max(s, axis=0, keepdims=True)
            cm = cm if off is None else cm + off
            col_max = cm if col_max is None else jnp.maximum(col_max, cm)
            offsets.append(off)
        return col_max, offsets

    def update(e, j, carry, scored):
        m, acc = carry
        col_max, offsets = scored
        m_new = jnp.maximum(m, col_max)
        acc = jnp.exp2(m - m_new) * acc
        for c in range(SLAB_CHUNKS):
            shift = m_new if offsets[c] is None else m_new - offsets[c]
            p = jnp.exp2(s_refs[e][c * SEQ_TILE:(c + 1) * SEQ_TILE, :] - shift)
            v_ones = jnp.concatenate([_value_t(vt_ref, j * SLAB_CHUNKS + c, e), ones], axis=0)
            acc = acc + _dot(v_ones, p.astype(BF16))
        return m_new, acc

    t = ATT_TILE
    init = (jnp.full((1, t), MASKED, F32), jnp.zeros((HEAD_DIM + ONES_ROWS, t), F32))
    last = jnp.maximum(own - 1, 0)

    def ring(j, carries, scored, diagonal, j_ahead):
        carries = list(carries)
        for e in range(group):
            ahead = score(e + 1, j, diagonal) if e + 1 < group else score(0, j_ahead, False)
            carries[e] = update(e, j, carries[e], scored)
            scored = ahead
        return tuple(carries), scored

    def past(j, state):
        return ring(j, state[0], state[1], False, jnp.minimum(j + 1, last))

    state = ring(own, (init,) * group, score(0, own, True), True, 0)
    quads = lax.shift_right_logical(own, 2)

    def four(n, st):
        for u in range(4):
            st = past(4 * n + u, st)
        return st

    state = lax.fori_loop(0, quads, four, state)
    carries, _ = lax.fori_loop(4 * quads, own, past, state)
    out_t = jnp.concatenate([acc[:HEAD_DIM] / acc[HEAD_DIM:HEAD_DIM + 1] for _, acc in carries], axis=0)
    o_ref[...] = out_t.T.astype(o_ref.dtype)


def _moba_kernel(q_ref, k_ref, vt_ref, sel_ref, o_ref, *s_refs):
    def selected(e, cg):
        return sel_ref[e, pl.ds(cg, 1), :]

    def diag_mask(e, c, cg, s):
        key, qry = _diag_positions(c)
        own = lax.shift_right_logical(qry, MOBA_BLOCK.bit_length() - 1) == c
        return jnp.where(own, jnp.where(key <= qry, s, MASKED), s + selected(e, cg))

    _softmax_attention(_packed_heads(q_ref, k_ref, len(s_refs)), vt_ref, o_ref, s_refs, diag_mask, selected)


def _mla_kernel(q_ref, k_ref, vt_ref, o_ref, *s_refs):
    def diag_mask(e, c, cg, s):
        key, qry = _diag_positions(c)
        return jnp.where(key <= qry, s, MASKED)

    _softmax_attention(_wide_heads(q_ref, k_ref, len(s_refs)), vt_ref, o_ref, s_refs, diag_mask, None)


def _log2_1m_beta(z2):
    return -(jnp.maximum(z2, 0.0) + jnp.log2(1.0 + jnp.exp2(-jnp.abs(z2))))


def _sb_kernel(q_ref, k_ref, vt_ref, up_ref, o_ref, a_ref, hi_ref, lo_ref):
    i = pl.program_id(0)
    t = SEQ_TILE
    upper = up_ref[...]
    lane = lax.broadcasted_iota(jnp.int32, (1, LANES), 1)
    qs = []
    for h in range(N_HEADS):
        q = q_ref[:, (h // PAIR) * LANES:(h // PAIR + 1) * LANES]
        qs.append(jnp.where((lane >= HEAD_DIM) if h % PAIR else (lane < HEAD_DIM), q, jnp.zeros_like(q)))

    def score(h, j, keep):
        z = _dot_nt(k_ref[_key_rows(j), (h // PAIR) * LANES:(h // PAIR + 1) * LANES], qs[h])
        log1m = _log2_1m_beta(z)
        if keep is not None:
            log1m = jnp.where(keep, log1m, 0.0)
        hi = log1m.astype(BF16)
        a_ref[h] = z + log1m
        hi_ref[h] = hi
        lo_ref[h] = (log1m - hi.astype(F32)).astype(BF16)
        return jnp.sum(log1m, axis=0, keepdims=True)

    def weigh(h, j, carry, col_sum, keep):
        c, acc = carry
        suffix = _dot(upper, hi_ref[h]) + _dot(upper, lo_ref[h]) + c
        wgt = jnp.exp2(a_ref[h] + suffix)
        if keep is not None:
            wgt = jnp.where(keep, wgt, 0.0)
        return c + col_sum, acc + _dot(_value_t(vt_ref, j, h), wgt.astype(BF16))

    def chunk(j, carry, diagonal):
        keep = None
        if diagonal:
            keep = lax.broadcasted_iota(jnp.int32, (t, t), 0) < lax.broadcasted_iota(jnp.int32, (t, t), 1)
        out = []
        col_sum = score(0, j, keep)
        for h in range(N_HEADS):
            nxt = score(h + 1, j, keep) if h + 1 < N_HEADS else None
            out.append(weigh(h, j, carry[h], col_sum, keep))
            col_sum = nxt
        return tuple(out)

    init = ((jnp.zeros((1, t), F32), jnp.zeros((HEAD_DIM, t), F32)),) * N_HEADS
    carry = chunk(i, init, True)

    def alive(state):
        n, carry = state
        c_max = carry[0][0]
        for h in range(1, N_HEADS):
            c_max = jnp.maximum(c_max, carry[h][0])
        return jnp.logical_and(n < i, jnp.max(c_max) > SB_EXP2_IS_ZERO)

    _, carry = lax.while_loop(alive, lambda st: (st[0] + 1, chunk(i - 1 - st[0], st[1], False)), (0, carry))
    o_ref[...] = jnp.concatenate([carry[h][1] for h in range(N_HEADS)], axis=0).T.astype(o_ref.dtype)


def _stick_breaking(q, k, vt, upper):
    s, w = q.shape
    t = SEQ_TILE
    return pl.pallas_call(
        _sb_kernel,
        grid=(s // t,),
        in_specs=[pl.BlockSpec((t, w), lambda i: (i, 0)), _const_spec((s, w)), _const_spec((s // t, w, t)),
                  _const_spec((t, t))],
        out_specs=pl.BlockSpec((t, w), lambda i: (i, 0)),
        out_shape=jax.ShapeDtypeStruct((s, w), BF16),
        scratch_shapes=[pltpu.VMEM((N_HEADS, t, t), F32), pltpu.VMEM((N_HEADS, t, t), BF16),
                        pltpu.VMEM((N_HEADS, t, t), BF16)],
        compiler_params=_params(1),
        name="stick_breaking_attention",
    )(q, k, vt, upper)


def _attention(kernel, q, k, vt, extra, extra_specs, group, group_lanes, name):
    s = q.shape[0]
    nb = s // SEQ_TILE
    rows = group * HEAD_DIM
    return pl.pallas_call(
        kernel,
        grid=(N_HEADS // group, s // ATT_TILE),
        in_specs=[
            pl.BlockSpec((ATT_TILE, group_lanes), lambda g, i: (i, g)),
            pl.BlockSpec((s, group_lanes), lambda g, i: (0, g), pipeline_mode=pl.Buffered(1)),
            pl.BlockSpec((nb, rows, SEQ_TILE), lambda g, i: (0, g, 0), pipeline_mode=pl.Buffered(1)),
        ] + extra_specs,
        out_specs=pl.BlockSpec((ATT_TILE, rows), lambda g, i: (i, g)),
        out_shape=jax.ShapeDtypeStruct((s, N_HEADS * HEAD_DIM), BF16),
        scratch_shapes=[pltpu.VMEM((ATT_TILE, ATT_TILE), F32)] * group,
        compiler_params=_params(2),
        name=name,
    )(q, k, vt, *extra)


def _merge_kernel(x_ref, ya_ref, yb_ref, yc_ref, wg_ref, bg_ref, wb_ref, wo_ref, mod_ref, gpre_ref, gpost_ref, o_ref):
    d = D_MODEL
    x = x_ref[...]
    shift, scale, gate = mod_ref[0:1, :], mod_ref[1:2, :], mod_ref[2:3, :]
    hn = (_rms_norm(x, gpre_ref[...]) * (1.0 + scale) + shift).astype(BF16)
    g = jax.nn.sigmoid(_dot(hn, wg_ref[...]) + bg_ref[...])
    merged = None
    for n, y_ref in enumerate((ya_ref, yb_ref, yc_ref)):
        part = _dot(y_ref[...], wb_ref[n]) * g[:, n * d:(n + 1) * d]
        merged = part if merged is None else merged + part
    out = _dot(merged.astype(BF16), wo_ref[...])
    o_ref[...] = x + (1.0 + gate) * _rms_norm(out, gpost_ref[...])


def _merge(x, ya, yb, yc, wp, mod, g_pre, g_post):
    s, d = x.shape
    t = ATT_TILE
    row_spec = lambda n: pl.BlockSpec((t, n), lambda i: (i, 0))
    return pl.pallas_call(
        _merge_kernel,
        grid=(s // t,),
        in_specs=[
            row_spec(d), row_spec(BRANCH_W), row_spec(BRANCH_W), row_spec(BRANCH_W),
            _const_spec(wp["w_g"].shape), _const_spec((1, N_BRANCH * d)), _const_spec(wp["w_b"].shape),
            _const_spec(wp["w_o"].shape), _const_spec((3, d)), _const_spec((1, d)), _const_spec((1, d)),
        ],
        out_specs=row_spec(d),
        out_shape=jax.ShapeDtypeStruct((s, d), F32),
        compiler_params=_params(1),
        name="mixer_merge",
    )(x, ya, yb, yc, wp["w_g"], wp["b_gate"], wp["w_b"], wp["w_o"], mod, g_pre.reshape(1, d), g_post.reshape(1, d))


def _rope_tables(n_rot, seq, first_lane, period):
    half = n_rot // 2
    assert first_lane % half == 0 and period % half == 0
    inv = jnp.power(jnp.float32(ROPE_THETA), -jnp.arange(0, n_rot, 2, dtype=F32) / n_rot)
    ang = jnp.arange(seq, dtype=F32)[:, None] * inv[None, :]
    cos, sin = jnp.tile(jnp.cos(ang), (1, LANES // half)), jnp.tile(jnp.sin(ang), (1, LANES // half))
    lane = jnp.arange(LANES) % period - first_lane
    lo = ((lane >= 0) & (lane < half))[None, :]
    hi = ((lane >= half) & (lane < n_rot))[None, :]
    return jnp.where(lo | hi, cos, 1.0), jnp.where(lo, -sin, 0.0), jnp.where(hi, sin, 0.0)


def _pad_heads(w, n_heads, width):
    k = w.shape[0]
    w = w.reshape(k, n_heads, width)
    return jnp.pad(w, ((0, 0), (0, 0), (0, HEAD_PAD - width))).reshape(k, n_heads * HEAD_PAD)


def _mixer_weights(w_in, b_gate, q_norm, w_uq, kv_norm, w_ukv, w_branch, w_out):
    w = BRANCH_W
    d = D_MODEL
    qa, ka, va, qb, kb, vb = (w_in[:, n * w:(n + 1) * w] for n in range(6))
    o = 6 * w
    w_cq = w_in[:, o:o + MLA_Q_LORA]
    o += MLA_Q_LORA
    w_ckv = w_in[:, o:o + MLA_KV_LORA]
    o += MLA_KV_LORA
    w_kr = jnp.pad(w_in[:, o:o + MLA_ROPE], ((0, 0), (MLA_NOPE, HEAD_PAD - MLA_QK)))
    o += MLA_ROPE
    w_gates = w_in[:, o:]
    ukv = w_ukv.reshape(MLA_KV_LORA, N_HEADS, MLA_NOPE + MLA_V)
    return {
        "w_row": jnp.concatenate([qa, ka, qb, kb, w_cq, w_ckv, w_kr], axis=1),
        "w_g": w_gates,
        "w_t": jnp.concatenate([va, vb], axis=1).T,
        "w_uq": _pad_heads(w_uq, N_HEADS, MLA_QK).astype(BF16),
        "w_uk": _pad_heads(ukv[:, :, :MLA_NOPE].reshape(MLA_KV_LORA, -1), N_HEADS, MLA_NOPE).astype(BF16),
        "w_uvt": ukv[:, :, MLA_NOPE:].reshape(MLA_KV_LORA, -1).T.astype(BF16),
        "q_norm": q_norm.reshape(1, -1),
        "kv_norm": kv_norm.reshape(1, -1),
        "b_gate": b_gate.reshape(1, N_BRANCH * d),
        "w_b": w_branch,
        "w_o": w_out,
    }


def kernel(x, c, ada_w, ada_b, norm_pre, norm_post, ffn_w_gate, ffn_w_up, ffn_w_down, mix_w_in, mix_b_gate,
           mla_q_norm, mla_w_uq, mla_kv_norm, mla_w_ukv, mix_w_branch, mix_w_out):
    bsz, s, d = x.shape
    assert bsz == 1 and d == D_MODEL and s % ATT_TILE == 0 and s % min(FFN_TILE, s) == 0
    depth = ada_w.shape[0]
    t = SEQ_TILE
    mod = _modulation(c, ada_w, ada_b)
    tabs = _rope_tables(PARTIAL_ROT, s, 0, HEAD_DIM) + _rope_tables(MLA_ROPE, s, MLA_NOPE, HEAD_PAD)
    upper = (jnp.arange(t)[None, :] > jnp.arange(t)[:, None]).astype(BF16)
    xs = x.reshape(s, d)
    wg, wu, wd = ffn_w_gate.astype(BF16), ffn_w_up.astype(BF16), ffn_w_down.astype(BF16)
    w_in, w_branch, w_out = mix_w_in.astype(BF16), mix_w_branch.astype(BF16), mix_w_out.astype(BF16)
    for l in range(depth):
        xs = _ffn(xs, mod[l, 0], norm_pre[l, 0], norm_post[l, 0], wg, wu, wd, l, 0, 0.5)
        wp = _mixer_weights(w_in[l], mix_b_gate[l], mla_q_norm[l], mla_w_uq[l], mla_kv_norm[l], mla_w_ukv[l],
                            w_branch[l], w_out[l])
        qa, ka, vat, sel, qb, kb, vbt, qc, kc, vct = _mixer_proj(xs, mod[l, 1], norm_pre[l, 1], wp, tabs)
        ya = _attention(_moba_kernel, qa, ka, vat, (sel,),
                        [pl.BlockSpec((MOBA_GROUP, s // t, ATT_TILE), lambda g, i: (g, 0, i))], MOBA_GROUP,
                        MOBA_GROUP // PAIR * LANES, "moba_attention")
        yb = _stick_breaking(qb, kb, vbt, upper)
        yc = _attention(_mla_kernel, qc, kc, vct, (), [], MLA_GROUP, MLA_GROUP * HEAD_PAD, "mla_attention")
        xs = _merge(xs, ya, yb, yc, wp, mod[l, 1], norm_pre[l, 1], norm_post[l, 1])
        xs = _ffn(xs, mod[l, 2], norm_pre[l, 2], norm_post[l, 2], wg, wu, wd, l, 1, 0.5)
    return xs.reshape(bsz, s, d)
```

```python
import functools

import jax
import jax.numpy as jnp
from jax import lax
from jax.experimental import pallas as pl
from jax.experimental.pallas import tpu as pltpu

F32 = jnp.float32
BF16 = jnp.bfloat16

D_MODEL = 1024
N_HEADS = 8
HEAD_DIM = 64
MOBA_BLOCK = 256
MOBA_TOPK = 3
MLA_Q_LORA = 256
MLA_KV_LORA = 128
MLA_NOPE = 64
MLA_ROPE = 32
MLA_V = 64
MLA_QK = MLA_NOPE + MLA_ROPE
ROPE_THETA = 500000.0
PARTIAL_ROT = HEAD_DIM // 4
D_FF = 2816
N_BRANCH = 3
BRANCH_W = N_HEADS * HEAD_DIM
N_SUB = 3
EPS = 1e-6

LANES = 128
V7X_VMEM_LIMIT = 56 * 1024 * 1024

SEQ_TILE = MOBA_BLOCK
FFN_TILE = 512
MOD_COLUMN_BLOCKS = 8
MASKED = -1e30
HEAD_PAD = LANES


def _dot(a, b):
    return jnp.dot(a, b, preferred_element_type=F32)


def _dot_nt(a, b, precision=None):
    return lax.dot_general(a, b, (((1,), (1,)), ((), ())), precision=precision, preferred_element_type=F32)


def _rms_norm(x, g):
    return x * lax.rsqrt(jnp.mean(x * x, axis=-1, keepdims=True) + EPS) * g


def _const_spec(shape):
    nd = len(shape)
    return pl.BlockSpec(shape, lambda *_: (0,) * nd, pipeline_mode=pl.Buffered(1))


def _params(n_grid):
    return pltpu.CompilerParams(dimension_semantics=("arbitrary",) * n_grid, vmem_limit_bytes=V7X_VMEM_LIMIT)


def _mod_kernel(c_ref, w_ref, b_ref, o_ref):
    c = c_ref[...]
    ca = c * jax.nn.sigmoid(c)
    o_ref[0] = jnp.sum(ca * w_ref[0], axis=0, keepdims=True) + b_ref[0]


def _modulation(c, ada_w, ada_b):
    depth, d, n = ada_w.shape
    tn = n // MOD_COLUMN_BLOCKS
    out = pl.pallas_call(
        _mod_kernel,
        grid=(depth, n // tn),
        in_specs=[
            pl.BlockSpec((d, 1), lambda l, j: (0, 0)),
            pl.BlockSpec((1, d, tn), lambda l, j: (l, 0, j)),
            pl.BlockSpec((1, 1, tn), lambda l, j: (l, 0, j)),
        ],
        out_specs=pl.BlockSpec((1, 1, tn), lambda l, j: (l, 0, j)),
        out_shape=jax.ShapeDtypeStruct((depth, 1, n), F32),
        compiler_params=_params(2),
        name="adaln_mod",
    )(c.reshape(d, 1), ada_w, ada_b.reshape(depth, 1, n))
    return out.reshape(depth, N_SUB, 3, d)


def _ffn_kernel(x_ref, mod_ref, gpre_ref, gpost_ref, wg_ref, wu_ref, wd_ref, o_ref, *, res_w):
    x = x_ref[...]
    shift, scale, gate = mod_ref[0:1, :], mod_ref[1:2, :], mod_ref[2:3, :]
    h = (_rms_norm(x, gpre_ref[...]) * (1.0 + scale) + shift).astype(BF16)
    g = _dot(h, wg_ref[...])
    u = _dot(h, wu_ref[...])
    a = (g * jax.nn.sigmoid(g) * u).astype(BF16)
    y = _dot(a, wd_ref[...])
    o_ref[...] = x + (res_w * (1.0 + gate)) * _rms_norm(y, gpost_ref[...])


def _ffn(x, mod, g_pre, g_post, wg, wu, wd, layer, half, res_w):
    s, d = x.shape
    tm = min(FFN_TILE, s)
    dff = wg.shape[-1]
    weight_spec = lambda shape: pl.BlockSpec((None, None) + shape, lambda i: (layer, half, 0, 0),
                                             pipeline_mode=pl.Buffered(1))
    return pl.pallas_call(
        functools.partial(_ffn_kernel, res_w=res_w),
        grid=(s // tm,),
        in_specs=[
            pl.BlockSpec((tm, d), lambda i: (i, 0)),
            _const_spec((3, d)),
            _const_spec((1, d)),
            _const_spec((1, d)),
            weight_spec((d, dff)),
            weight_spec((d, dff)),
            weight_spec((dff, d)),
        ],
        out_specs=pl.BlockSpec((tm, d), lambda i: (i, 0)),
        out_shape=jax.ShapeDtypeStruct((s, d), F32),
        compiler_params=_params(1),
        name="macaron_ffn",
    )(x, mod, g_pre.reshape(1, d), g_post.reshape(1, d), wg, wu, wd)


def _rope_rows(x, a, b, c, half):
    outs = []
    for g in range(x.shape[1] // LANES):
        xg = x[:, g * LANES:(g + 1) * LANES]
        outs.append(xg * a + pltpu.roll(xg, LANES - half, 1) * b + pltpu.roll(xg, half, 1) * c)
    return jnp.concatenate(outs, axis=1) if len(outs) > 1 else outs[0]


def _proj_kernel(x_ref, mod_ref, gpre_ref, wrow_ref, wuq_ref, wuk_ref, wuvt_ref, qn_ref, kvn_ref,
                 pa_ref, pb_ref, pc_ref, ma_ref, mb_ref, mc_ref,
                 qa_ref, ka_ref, vat_ref, sel_ref, qb_ref, kb_ref, vbt_ref, qc_ref, kc_ref, vct_ref,
                 kmean_ref, *, n_blocks):
    i = pl.program_id(0)
    w = BRANCH_W
    x = x_ref[...]
    shift, scale = mod_ref[0:1, :], mod_ref[1:2, :]
    hn = (_rms_norm(x, gpre_ref[...]) * (1.0 + scale) + shift).astype(BF16)
    rows = _dot(hn, wrow_ref[...])
    cols_t = rows[:, rows.shape[1] - 2 * w:].T

    pa, pb, pc = pa_ref[...], pb_ref[...], pc_ref[...]
    q_a = _rope_rows(rows[:, 0:w], pa, pb, pc, PARTIAL_ROT // 2)
    k_a = _rope_rows(rows[:, w:2 * w], pa, pb, pc, PARTIAL_ROT // 2)
    qa_ref[...] = (q_a * (HEAD_DIM ** -0.5 * LOG2E)).astype(BF16)
    ka_ref[...] = k_a.astype(BF16)
    vat_ref[0] = cols_t[0:w, :].astype(BF16)

    @pl.when(i == 0)
    def _():
        kmean_ref[...] = jnp.zeros_like(kmean_ref)

    kmean_ref[pl.ds(i, 1), :] = jnp.mean(k_a, axis=0, keepdims=True)
    kmean = kmean_ref[...]
    t = x.shape[0]
    lane = lax.broadcasted_iota(jnp.int32, (1, LANES), 1)
    blk = lax.broadcasted_iota(jnp.int32, (n_blocks, t), 0)
    past = blk < i
    blk_f = blk.astype(F32)
    for h in range(N_HEADS):
        p, e = h // 2, h % 2
        if e == 0:
            q_pair = q_a[:, p * LANES:(p + 1) * LANES]
            qm = jnp.concatenate([jnp.where(lane < HEAD_DIM, q_pair, 0.0), jnp.where(lane >= HEAD_DIM, q_pair, 0.0)],
                                 axis=0)
            gates = _dot_nt(kmean[:, p * LANES:(p + 1) * LANES], qm, precision=lax.Precision.HIGHEST)
        gate = jnp.where(past, gates[:, e * t:(e + 1) * t], -jnp.inf)
        chosen = jnp.zeros(gate.shape, F32)
        for _ in range(min(MOBA_TOPK, n_blocks)):
            top = jnp.max(gate, axis=0, keepdims=True)
            first = jnp.min(jnp.where(gate == top, blk_f, float(n_blocks)), axis=0, keepdims=True)
            pick = blk_f == first
            chosen = jnp.where(pick, 1.0, chosen)
            gate = jnp.where(pick, -jnp.inf, gate)
        sel_ref[h] = jnp.where(jnp.logical_and(chosen > 0.0, past), 0.0, MASKED)

    qb_ref[...] = (rows[:, 2 * w:3 * w] * (HEAD_DIM ** -0.5 * LOG2E)).astype(BF16)
    kb_ref[...] = rows[:, 3 * w:4 * w].astype(BF16)
    vbt_ref[0] = cols_t[w:2 * w, :].astype(BF16)

    o = 4 * w
    ma, mb, mc = ma_ref[...], mb_ref[...], mc_ref[...]
    cq = _rms_norm(rows[:, o:o + MLA_Q_LORA], qn_ref[...]).astype(BF16)
    q_c = _rope_rows(_dot(cq, wuq_ref[...]), ma, mb, mc, MLA_ROPE // 2)
    qc_ref[...] = (q_c * (MLA_QK ** -0.5 * LOG2E)).astype(BF16)
    o += MLA_Q_LORA
    ckv = _rms_norm(rows[:, o:o + MLA_KV_LORA], kvn_ref[...]).astype(BF16)
    o += MLA_KV_LORA
    k_rope = _rope_rows(rows[:, o:o + HEAD_PAD], ma, mb, mc, MLA_ROPE // 2)
    kc_ref[...] = (_dot(ckv, wuk_ref[...]) + jnp.concatenate([k_rope] * N_HEADS, axis=1)).astype(BF16)
    vct_ref[0] = _dot_nt(wuvt_ref[...], ckv).astype(BF16)


def _mixer_proj(x, mod, g_pre, wp, tabs):
    s, d = x.shape
    t = SEQ_TILE
    nb = s // t
    w = BRANCH_W
    hp = N_HEADS * HEAD_PAD
    row_spec = lambda n: pl.BlockSpec((t, n), lambda i: (i, 0))
    vt_spec = pl.BlockSpec((1, w, t), lambda i: (i, 0, 0))
    tab_spec = pl.BlockSpec((t, LANES), lambda i: (i, 0))
    out_shape = (
        jax.ShapeDtypeStruct((s, w), BF16), jax.ShapeDtypeStruct((s, w), BF16), jax.ShapeDtypeStruct((nb, w, t), BF16),
        jax.ShapeDtypeStruct((N_HEADS, nb, s), F32),
        jax.ShapeDtypeStruct((s, w), BF16), jax.ShapeDtypeStruct((s, w), BF16), jax.ShapeDtypeStruct((nb, w, t), BF16),
        jax.ShapeDtypeStruct((s, hp), BF16), jax.ShapeDtypeStruct((s, hp), BF16), jax.ShapeDtypeStruct((nb, w, t), BF16),
    )
    out_specs = (
        row_spec(w), row_spec(w), vt_spec,
        pl.BlockSpec((N_HEADS, nb, t), lambda i: (0, 0, i)),
        row_spec(w), row_spec(w), vt_spec,
        row_spec(hp), row_spec(hp), vt_spec,
    )
    in_specs = [
        pl.BlockSpec((t, d), lambda i: (i, 0)),
        _const_spec((3, d)), _const_spec((1, d)),
        _const_spec(wp["w_row"].shape), _const_spec(wp["w_uq"].shape),
        _const_spec(wp["w_uk"].shape), _const_spec(wp["w_uvt"].shape),
        _const_spec((1, MLA_Q_LORA)), _const_spec((1, MLA_KV_LORA)),
    ] + [tab_spec] * 6
    return pl.pallas_call(
        functools.partial(_proj_kernel, n_blocks=nb),
        grid=(nb,),
        in_specs=in_specs,
        out_specs=out_specs,
        out_shape=out_shape,
        scratch_shapes=[pltpu.VMEM((nb, w), F32)],
        compiler_params=_params(1),
        name="mixer_proj",
    )(x, mod, g_pre.reshape(1, d), wp["w_row"], wp["w_uq"], wp["w_uk"], wp["w_uvt"],
      wp["q_norm"], wp["kv_norm"], *tabs)


ATT_TILE = 512
SLAB_CHUNKS = ATT_TILE // SEQ_TILE
PAIR = 2
MOBA_GROUP = 8
MLA_GROUP = 4
ONES_ROWS = 16
LOG2E = 1.4426950408889634
SB_EXP2_IS_ZERO = -150.0


def _key_rows(c):
    return pl.ds(pl.multiple_of(c * SEQ_TILE, SEQ_TILE), SEQ_TILE)


def _slab_rows(j):
    return pl.ds(pl.multiple_of(j * ATT_TILE, ATT_TILE), ATT_TILE)


def _packed_heads(q_ref, k_ref, group):
    lane = lax.broadcasted_iota(jnp.int32, (1, LANES), 1)
    qs = []
    for e in range(group):
        q = q_ref[:, (e // PAIR) * LANES:(e // PAIR + 1) * LANES]
        qs.append(jnp.where((lane >= HEAD_DIM) if e % PAIR else (lane < HEAD_DIM), q, jnp.zeros_like(q)))
    return qs, lambda j, e: k_ref[_slab_rows(j), (e // PAIR) * LANES:(e // PAIR + 1) * LANES]


def _wide_heads(q_ref, k_ref, group):
    qs = [q_ref[:, e * LANES:(e + 1) * LANES] for e in range(group)]
    return qs, lambda j, e: k_ref[_slab_rows(j), e * LANES:(e + 1) * LANES]


def _value_t(vt_ref, c, e):
    return vt_ref[c, e * HEAD_DIM:(e + 1) * HEAD_DIM, :]


def _diag_positions(c):
    key = lax.broadcasted_iota(jnp.int32, (SEQ_TILE, ATT_TILE), 0) + c * SEQ_TILE
    qry = lax.broadcasted_iota(jnp.int32, (SEQ_TILE, ATT_TILE), 1)
    return key, qry


def _softmax_attention(q_pair, vt_ref, o_ref, s_refs, diag_mask, past_offset):
    qs, keys = q_pair
    group = len(qs)
    own = pl.program_id(1)
    ones = jnp.ones((ONES_ROWS, SEQ_TILE), BF16)

    def score(e, j, diagonal):
        col_max, offsets = None, []
        slab = _dot_nt(keys(j, e), qs[e])
        for c in range(SLAB_CHUNKS):
            cg = j * SLAB_CHUNKS + c
            s = slab[c * SEQ_TILE:(c + 1) * SEQ_TILE, :]
            off = None
            if diagonal:
                s = diag_mask(e, c, cg, s)
            elif past_offset is not None:
                off = past_offset(e, cg)
            s_refs[e][c * SEQ_TILE:(c + 1) * SEQ_TILE, :] = s
            cm = jnp.max(s, axis=0, keepdims=True)
            cm = cm if off is None else cm + off
            col_max = cm if col_max is None else jnp.maximum(col_max, cm)
            offsets.append(off)
        return col_max, offsets

    def update(e, j, carry, scored):
        m, acc = carry
        col_max, offsets = scored
        m_new = jnp.maximum(m, col_max)
        acc = jnp.exp2(m - m_new) * acc
        for c in range(SLAB_CHUNKS):
            shift = m_new if offsets[c] is None else m_new - offsets[c]
            p = jnp.exp2(s_refs[e][c * SEQ_TILE:(c + 1) * SEQ_TILE, :] - shift)
            v_ones = jnp.concatenate([_value_t(vt_ref, j * SLAB_CHUNKS + c, e), ones], axis=0)
            acc = acc + _dot(v_ones, p.astype(BF16))
        return m_new, acc

    t = ATT_TILE
    init = (jnp.full((1, t), MASKED, F32), jnp.zeros((HEAD_DIM + ONES_ROWS, t), F32))
    last = jnp.maximum(own - 1, 0)

    def ring(j, carries, scored, diagonal, j_ahead):
        carries = list(carries)
        for e in range(group):
            ahead = score(e + 1, j, diagonal) if e + 1 < group else score(0, j_ahead, False)
            carries[e] = update(e, j, carries[e], scored)
            scored = ahead
        return tuple(carries), scored

    def past(j, state):
        return ring(j, state[0], state[1], False, jnp.minimum(j + 1, last))

    state = ring(own, (init,) * group, score(0, own, True), True, 0)
    quads = lax.shift_right_logical(own, 2)

    def four(n, st):
        for u in range(4):
            st = past(4 * n + u, st)
        return st

    state = lax.fori_loop(0, quads, four, state)
    carries, _ = lax.fori_loop(4 * quads, own, past, state)
    out_t = jnp.concatenate([acc[:HEAD_DIM] / acc[HEAD_DIM:HEAD_DIM + 1] for _, acc in carries], axis=0)
    o_ref[...] = out_t.T.astype(o_ref.dtype)


def _moba_kernel(q_ref, k_ref, vt_ref, sel_ref, o_ref, *s_refs):
    def selected(e, cg):
        return sel_ref[e, pl.ds(cg, 1), :]

    def diag_mask(e, c, cg, s):
        key, qry = _diag_positions(c)
        own = lax.shift_right_logical(qry, MOBA_BLOCK.bit_length() - 1) == c
        return jnp.where(own, jnp.where(key <= qry, s, MASKED), s + selected(e, cg))

    _softmax_attention(_packed_heads(q_ref, k_ref, len(s_refs)), vt_ref, o_ref, s_refs, diag_mask, selected)


def _mla_kernel(q_ref, k_ref, vt_ref, o_ref, *s_refs):
    def diag_mask(e, c, cg, s):
        key, qry = _diag_positions(c)
        return jnp.where(key <= qry, s, MASKED)

    _softmax_attention(_wide_heads(q_ref, k_ref, len(s_refs)), vt_ref, o_ref, s_refs, diag_mask, None)


def _log2_1m_beta(z2):
    return -(jnp.maximum(z2, 0.0) + jnp.log2(1.0 + jnp.exp2(-jnp.abs(z2))))


def _sb_kernel(q_ref, k_ref, vt_ref, up_ref, o_ref, a_ref, hi_ref, lo_ref):
    i = pl.program_id(0)
    t = SEQ_TILE
    upper = up_ref[...]
    lane = lax.broadcasted_iota(jnp.int32, (1, LANES), 1)
    qs = []
    for h in range(N_HEADS):
        q = q_ref[:, (h // PAIR) * LANES:(h // PAIR + 1) * LANES]
        qs.append(jnp.where((lane >= HEAD_DIM) if h % PAIR else (lane < HEAD_DIM), q, jnp.zeros_like(q)))

    def score(h, j, keep):
        z = _dot_nt(k_ref[_key_rows(j), (h // PAIR) * LANES:(h // PAIR + 1) * LANES], qs[h])
        log1m = _log2_1m_beta(z)
        if keep is not None:
            log1m = jnp.where(keep, log1m, 0.0)
        hi = log1m.astype(BF16)
        a_ref[h] = z + log1m
        hi_ref[h] = hi
        lo_ref[h] = (log1m - hi.astype(F32)).astype(BF16)
        return jnp.sum(log1m, axis=0, keepdims=True)

    def weigh(h, j, carry, col_sum, keep):
        c, acc = carry
        suffix = _dot(upper, hi_ref[h]) + _dot(upper, lo_ref[h]) + c
        wgt = jnp.exp2(a_ref[h] + suffix)
        if keep is not None:
            wgt = jnp.where(keep, wgt, 0.0)
        return c + col_sum, acc + _dot(_value_t(vt_ref, j, h), wgt.astype(BF16))

    def chunk(j, carry, diagonal):
        keep = None
        if diagonal:
            keep = lax.broadcasted_iota(jnp.int32, (t, t), 0) < lax.broadcasted_iota(jnp.int32, (t, t), 1)
        out = []
        col_sum = score(0, j, keep)
        for h in range(N_HEADS):
            nxt = score(h + 1, j, keep) if h + 1 < N_HEADS else None
            out.append(weigh(h, j, carry[h], col_sum, keep))
            col_sum = nxt
        return tuple(out)

    init = ((jnp.zeros((1, t), F32), jnp.zeros((HEAD_DIM, t), F32)),) * N_HEADS
    carry = chunk(i, init, True)

    def alive(state):
        n, carry = state
        c_max = carry[0][0]
        for h in range(1, N_HEADS):
            c_max = jnp.maximum(c_max, carry[h][0])
        return jnp.logical_and(n < i, jnp.max(c_max) > SB_EXP2_IS_ZERO)

    _, carry = lax.while_loop(alive, lambda st: (st[0] + 1, chunk(i - 1 - st[0], st[1], False)), (0, carry))
    o_ref[...] = jnp.concatenate([carry[h][1] for h in range(N_HEADS)], axis=0).T.astype(o_ref.dtype)


def _stick_breaking(q, k, vt, upper):
    s, w = q.shape
    t = SEQ_TILE
    return pl.pallas_call(
        _sb_kernel,
        grid=(s // t,),
        in_specs=[pl.BlockSpec((t, w), lambda i: (i, 0)), _const_spec((s, w)), _const_spec((s // t, w, t)),
                  _const_spec((t, t))],
        out_specs=pl.BlockSpec((t, w), lambda i: (i, 0)),
        out_shape=jax.ShapeDtypeStruct((s, w), BF16),
        scratch_shapes=[pltpu.VMEM((N_HEADS, t, t), F32), pltpu.VMEM((N_HEADS, t, t), BF16),
                        pltpu.VMEM((N_HEADS, t, t), BF16)],
        compiler_params=_params(1),
        name="stick_breaking_attention",
    )(q, k, vt, upper)


def _attention(kernel, q, k, vt, extra, extra_specs, group, group_lanes, name):
    s = q.shape[0]
    nb = s // SEQ_TILE
    rows = group * HEAD_DIM
    return pl.pallas_call(
        kernel,
        grid=(N_HEADS // group, s // ATT_TILE),
        in_specs=[
            pl.BlockSpec((ATT_TILE, group_lanes), lambda g, i: (i, g)),
            pl.BlockSpec((s, group_lanes), lambda g, i: (0, g), pipeline_mode=pl.Buffered(1)),
            pl.BlockSpec((nb, rows, SEQ_TILE), lambda g, i: (0, g, 0), pipeline_mode=pl.Buffered(1)),
        ] + extra_specs,
        out_specs=pl.BlockSpec((ATT_TILE, rows), lambda g, i: (i, g)),
        out_shape=jax.ShapeDtypeStruct((s, N_HEADS * HEAD_DIM), BF16),
        scratch_shapes=[pltpu.VMEM((ATT_TILE, ATT_TILE), F32)] * group,
        compiler_params=_params(2),
        name=name,
    )(q, k, vt, *extra)


def _merge_kernel(x_ref, ya_ref, yb_ref, yc_ref, wg_ref, bg_ref, wb_ref, wo_ref, mod_ref, gpre_ref, gpost_ref, o_ref):
    d = D_MODEL
    x = x_ref[...]
    shift, scale, gate = mod_ref[0:1, :], mod_ref[1:2, :], mod_ref[2:3, :]
    hn = (_rms_norm(x, gpre_ref[...]) * (1.0 + scale) + shift).astype(BF16)
    g = jax.nn.sigmoid(_dot(hn, wg_ref[...]) + bg_ref[...])
    merged = None
    for n, y_ref in enumerate((ya_ref, yb_ref, yc_ref)):
        part = _dot(y_ref[...], wb_ref[n]) * g[:, n * d:(n + 1) * d]
        merged = part if merged is None else merged + part
    out = _dot(merged.astype(BF16), wo_ref[...])
    o_ref[...] = x + (1.0 + gate) * _rms_norm(out, gpost_ref[...])


def _merge(x, ya, yb, yc, wp, mod, g_pre, g_post):
    s, d = x.shape
    t = ATT_TILE
    row_spec = lambda n: pl.BlockSpec((t, n), lambda i: (i, 0))
    return pl.pallas_call(
        _merge_kernel,
        grid=(s // t,),
        in_specs=[
            row_spec(d), row_spec(BRANCH_W), row_spec(BRANCH_W), row_spec(BRANCH_W),
            _const_spec(wp["w_g"].shape), _const_spec((1, N_BRANCH * d)), _const_spec(wp["w_b"].shape),
            _const_spec(wp["w_o"].shape), _const_spec((3, d)), _const_spec((1, d)), _const_spec((1, d)),
        ],
        out_specs=row_spec(d),
        out_shape=jax.ShapeDtypeStruct((s, d), F32),
        compiler_params=_params(1),
        name="mixer_merge",
    )(x, ya, yb, yc, wp["w_g"], wp["b_gate"], wp["w_b"], wp["w_o"], mod, g_pre.reshape(1, d), g_post.reshape(1, d))


def _rope_tables(n_rot, seq, first_lane, period):
    half = n_rot // 2
    assert first_lane % half == 0 and period % half == 0
    inv = jnp.power(jnp.float32(ROPE_THETA), -jnp.arange(0, n_rot, 2, dtype=F32) / n_rot)
    ang = jnp.arange(seq, dtype=F32)[:, None] * inv[None, :]
    cos, sin = jnp.tile(jnp.cos(ang), (1, LANES // half)), jnp.tile(jnp.sin(ang), (1, LANES // half))
    lane = jnp.arange(LANES) % period - first_lane
    lo = ((lane >= 0) & (lane < half))[None, :]
    hi = ((lane >= half) & (lane < n_rot))[None, :]
    return jnp.where(lo | hi, cos, 1.0), jnp.where(lo, -sin, 0.0), jnp.where(hi, sin, 0.0)


def _pad_heads(w, n_heads, width):
    k = w.shape[0]
    w = w.reshape(k, n_heads, width)
    return jnp.pad(w, ((0, 0), (0, 0), (0, HEAD_PAD - width))).reshape(k, n_heads * HEAD_PAD)


def _mixer_weights(w_in, b_gate, q_norm, w_uq, kv_norm, w_ukv, w_branch, w_out):
    w = BRANCH_W
    d = D_MODEL
    qa, ka, va, qb, kb, vb = (w_in[:, n * w:(n + 1) * w] for n in range(6))
    o = 6 * w
    w_cq = w_in[:, o:o + MLA_Q_LORA]
    o += MLA_Q_LORA
    w_ckv = w_in[:, o:o + MLA_KV_LORA]
    o += MLA_KV_LORA
    w_kr = jnp.pad(w_in[:, o:o + MLA_ROPE], ((0, 0), (MLA_NOPE, HEAD_PAD - MLA_QK)))
    o += MLA_ROPE
    w_gates = w_in[:, o:]
    ukv = w_ukv.reshape(MLA_KV_LORA, N_HEADS, MLA_NOPE + MLA_V)
    return {
        "w_row": jnp.concatenate([qa, ka, qb, kb, w_cq, w_ckv, w_kr, va, vb], axis=1),
        "w_g": w_gates,
        "w_uq": _pad_heads(w_uq, N_HEADS, MLA_QK).astype(BF16),
        "w_uk": _pad_heads(ukv[:, :, :MLA_NOPE].reshape(MLA_KV_LORA, -1), N_HEADS, MLA_NOPE).astype(BF16),
        "w_uvt": ukv[:, :, MLA_NOPE:].reshape(MLA_KV_LORA, -1).T.astype(BF16),
        "q_norm": q_norm.reshape(1, -1),
        "kv_norm": kv_norm.reshape(1, -1),
        "b_gate": b_gate.reshape(1, N_BRANCH * d),
        "w_b": w_branch,
        "w_o": w_out,
    }


def kernel(x, c, ada_w, ada_b, norm_pre, norm_post, ffn_w_gate, ffn_w_up, ffn_w_down, mix_w_in, mix_b_gate,
           mla_q_norm, mla_w_uq, mla_kv_norm, mla_w_ukv, mix_w_branch, mix_w_out):
    bsz, s, d = x.shape
    assert bsz == 1 and d == D_MODEL and s % ATT_TILE == 0 and s % min(FFN_TILE, s) == 0
    depth = ada_w.shape[0]
    t = SEQ_TILE
    mod = _modulation(c, ada_w, ada_b)
    tabs = _rope_tables(PARTIAL_ROT, s, 0, HEAD_DIM) + _rope_tables(MLA_ROPE, s, MLA_NOPE, HEAD_PAD)
    upper = (jnp.arange(t)[None, :] > jnp.arange(t)[:, None]).astype(BF16)
    xs = x.reshape(s, d)
    wg, wu, wd = ffn_w_gate.astype(BF16), ffn_w_up.astype(BF16), ffn_w_down.astype(BF16)
    w_in, w_branch, w_out = mix_w_in.astype(BF16), mix_w_branch.astype(BF16), mix_w_out.astype(BF16)
    for l in range(depth):
        xs = _ffn(xs, mod[l, 0], norm_pre[l, 0], norm_post[l, 0], wg, wu, wd, l, 0, 0.5)
        wp = _mixer_weights(w_in[l], mix_b_gate[l], mla_q_norm[l], mla_w_uq[l], mla_kv_norm[l], mla_w_ukv[l],
                            w_branch[l], w_out[l])
        qa, ka, vat, sel, qb, kb, vbt, qc, kc, vct = _mixer_proj(xs, mod[l, 1], norm_pre[l, 1], wp, tabs)
        ya = _attention(_moba_kernel, qa, ka, vat, (sel,),
                        [pl.BlockSpec((MOBA_GROUP, s // t, ATT_TILE), lambda g, i: (g, 0, i))], MOBA_GROUP,
                        MOBA_GROUP // PAIR * LANES, "moba_attention")
        yb = _stick_breaking(qb, kb, vbt, upper)
        yc = _attention(_mla_kernel, qc, kc, vct, (), [], MLA_GROUP, MLA_GROUP * HEAD_PAD, "mla_attention")
        xs = _merge(xs, ya, yb, yc, wp, mod[l, 1], norm_pre[l, 1], norm_post[l, 1])
        xs = _ffn(xs, mod[l, 2], norm_pre[l, 2], norm_post[l, 2], wg, wu, wd, l, 1, 0.5)
    return xs.reshape(bsz, s, d)
```

```python
import functools

import jax
import jax.numpy as jnp
from jax import lax
from jax.experimental import pallas as pl
from jax.experimental.pallas import tpu as pltpu

F32 = jnp.float32
BF16 = jnp.bfloat16

D_MODEL = 1024
N_HEADS = 8
HEAD_DIM = 64
MOBA_BLOCK = 256
MOBA_TOPK = 3
MLA_Q_LORA = 256
MLA_KV_LORA = 128
MLA_NOPE = 64
MLA_ROPE = 32
MLA_V = 64
MLA_QK = MLA_NOPE + MLA_ROPE
ROPE_THETA = 500000.0
PARTIAL_ROT = HEAD_DIM // 4
D_FF = 2816
N_BRANCH = 3
BRANCH_W = N_HEADS * HEAD_DIM
N_SUB = 3
EPS = 1e-6

LANES = 128
V7X_VMEM_LIMIT = 56 * 1024 * 1024

SEQ_TILE = MOBA_BLOCK
FFN_TILE = 512
MOD_COLUMN_BLOCKS = 8
MASKED = -1e30
HEAD_PAD = LANES


def _dot(a, b):
    return jnp.dot(a, b, preferred_element_type=F32)


def _dot_nt(a, b):
    return lax.dot_general(a, b, (((1,), (1,)), ((), ())), preferred_element_type=F32)


def _dot_nt_3pass(a, b):
    a_hi, b_hi = a.astype(BF16), b.astype(BF16)
    a_lo, b_lo = (a - a_hi.astype(F32)).astype(BF16), (b - b_hi.astype(F32)).astype(BF16)
    return _dot_nt(a_hi, b_hi) + _dot_nt(a_hi, b_lo) + _dot_nt(a_lo, b_hi)


def _rms_norm(x, g):
    return x * lax.rsqrt(jnp.mean(x * x, axis=-1, keepdims=True) + EPS) * g


def _const_spec(shape):
    nd = len(shape)
    return pl.BlockSpec(shape, lambda *_: (0,) * nd, pipeline_mode=pl.Buffered(1))


def _params(n_grid):
    return pltpu.CompilerParams(dimension_semantics=("arbitrary",) * n_grid, vmem_limit_bytes=V7X_VMEM_LIMIT)


def _mod_kernel(c_ref, w_ref, b_ref, o_ref):
    c = c_ref[...]
    ca = c * jax.nn.sigmoid(c)
    o_ref[0] = jnp.sum(ca * w_ref[0], axis=0, keepdims=True) + b_ref[0]


def _modulation(c, ada_w, ada_b):
    depth, d, n = ada_w.shape
    tn = n // MOD_COLUMN_BLOCKS
    out = pl.pallas_call(
        _mod_kernel,
        grid=(depth, n // tn),
        in_specs=[
            pl.BlockSpec((d, 1), lambda l, j: (0, 0)),
            pl.BlockSpec((1, d, tn), lambda l, j: (l, 0, j)),
            pl.BlockSpec((1, 1, tn), lambda l, j: (l, 0, j)),
        ],
        out_specs=pl.BlockSpec((1, 1, tn), lambda l, j: (l, 0, j)),
        out_shape=jax.ShapeDtypeStruct((depth, 1, n), F32),
        compiler_params=_params(2),
        name="adaln_mod",
    )(c.reshape(d, 1), ada_w, ada_b.reshape(depth, 1, n))
    return out.reshape(depth, N_SUB, 3, d)


def _ffn_kernel(x_ref, mod_ref, gpre_ref, gpost_ref, wg_ref, wu_ref, wd_ref, o_ref, *, res_w):
    x = x_ref[...]
    shift, scale, gate = mod_ref[0:1, :], mod_ref[1:2, :], mod_ref[2:3, :]
    h = (_rms_norm(x, gpre_ref[...]) * (1.0 + scale) + shift).astype(BF16)
    g = _dot(h, wg_ref[...])
    u = _dot(h, wu_ref[...])
    a = (g * jax.nn.sigmoid(g) * u).astype(BF16)
    y = _dot(a, wd_ref[...])
    o_ref[...] = x + (res_w * (1.0 + gate)) * _rms_norm(y, gpost_ref[...])


def _ffn(x, mod, g_pre, g_post, wg, wu, wd, layer, half, res_w):
    s, d = x.shape
    tm = min(FFN_TILE, s)
    dff = wg.shape[-1]
    weight_spec = lambda shape: pl.BlockSpec((None, None) + shape, lambda i: (layer, half, 0, 0),
                                             pipeline_mode=pl.Buffered(1))
    return pl.pallas_call(
        functools.partial(_ffn_kernel, res_w=res_w),
        grid=(s // tm,),
        in_specs=[
            pl.BlockSpec((tm, d), lambda i: (i, 0)),
            _const_spec((3, d)),
            _const_spec((1, d)),
            _const_spec((1, d)),
            weight_spec((d, dff)),
            weight_spec((d, dff)),
            weight_spec((dff, d)),
        ],
        out_specs=pl.BlockSpec((tm, d), lambda i: (i, 0)),
        out_shape=jax.ShapeDtypeStruct((s, d), F32),
        compiler_params=_params(1),
        name="macaron_ffn",
    )(x, mod, g_pre.reshape(1, d), g_post.reshape(1, d), wg, wu, wd)


def _rope_rows(x, a, b, c, half):
    outs = []
    for g in range(x.shape[1] // LANES):
        xg = x[:, g * LANES:(g + 1) * LANES]
        outs.append(xg * a + pltpu.roll(xg, LANES - half, 1) * b + pltpu.roll(xg, half, 1) * c)
    return jnp.concatenate(outs, axis=1) if len(outs) > 1 else outs[0]


def _proj_kernel(x_ref, mod_ref, gpre_ref, wrow_ref, wuq_ref, wuk_ref, wuvt_ref, qn_ref, kvn_ref,
                 pa_ref, pb_ref, pc_ref, ma_ref, mb_ref, mc_ref,
                 qa_ref, ka_ref, vat_ref, sel_ref, qb_ref, kb_ref, vbt_ref, qc_ref, kc_ref, vct_ref,
                 kmean_ref, *, n_blocks):
    i = pl.program_id(0)
    w = BRANCH_W
    x = x_ref[...]
    shift, scale = mod_ref[0:1, :], mod_ref[1:2, :]
    hn = (_rms_norm(x, gpre_ref[...]) * (1.0 + scale) + shift).astype(BF16)
    rows = _dot(hn, wrow_ref[...])
    cols_t = rows[:, rows.shape[1] - 2 * w:].T

    pa, pb, pc = pa_ref[...], pb_ref[...], pc_ref[...]
    q_a = _rope_rows(rows[:, 0:w], pa, pb, pc, PARTIAL_ROT // 2)
    k_a = _rope_rows(rows[:, w:2 * w], pa, pb, pc, PARTIAL_ROT // 2)
    qa_ref[...] = (q_a * (HEAD_DIM ** -0.5 * LOG2E)).astype(BF16)
    ka_ref[...] = k_a.astype(BF16)
    vat_ref[0] = cols_t[0:w, :].astype(BF16)

    @pl.when(i == 0)
    def _():
        kmean_ref[...] = jnp.zeros_like(kmean_ref)

    kmean_ref[pl.ds(i, 1), :] = jnp.mean(k_a, axis=0, keepdims=True)
    kmean = kmean_ref[...]
    t = x.shape[0]
    lane = lax.broadcasted_iota(jnp.int32, (1, LANES), 1)
    blk = lax.broadcasted_iota(jnp.int32, (n_blocks, t), 0)
    past = blk < i
    blk_f = blk.astype(F32)
    for h in range(N_HEADS):
        p, e = h // 2, h % 2
        if e == 0:
            q_pair = q_a[:, p * LANES:(p + 1) * LANES]
            qm = jnp.concatenate([jnp.where(lane < HEAD_DIM, q_pair, 0.0), jnp.where(lane >= HEAD_DIM, q_pair, 0.0)],
                                 axis=0)
            gates = _dot_nt_3pass(kmean[:, p * LANES:(p + 1) * LANES], qm)
        gate = jnp.where(past, gates[:, e * t:(e + 1) * t], -jnp.inf)
        chosen = jnp.zeros(gate.shape, F32)
        for _ in range(min(MOBA_TOPK, n_blocks)):
            top = jnp.max(gate, axis=0, keepdims=True)
            first = jnp.min(jnp.where(gate == top, blk_f, float(n_blocks)), axis=0, keepdims=True)
            pick = blk_f == first
            chosen = jnp.where(pick, 1.0, chosen)
            gate = jnp.where(pick, -jnp.inf, gate)
        sel_ref[h] = jnp.where(jnp.logical_and(chosen > 0.0, past), 0.0, MASKED)

    qb_ref[...] = (rows[:, 2 * w:3 * w] * (HEAD_DIM ** -0.5 * LOG2E)).astype(BF16)
    kb_ref[...] = rows[:, 3 * w:4 * w].astype(BF16)
    vbt_ref[0] = cols_t[w:2 * w, :].astype(BF16)

    o = 4 * w
    ma, mb, mc = ma_ref[...], mb_ref[...], mc_ref[...]
    cq = _rms_norm(rows[:, o:o + MLA_Q_LORA], qn_ref[...]).astype(BF16)
    q_c = _rope_rows(_dot(cq, wuq_ref[...]), ma, mb, mc, MLA_ROPE // 2)
    qc_ref[...] = (q_c * (MLA_QK ** -0.5 * LOG2E)).astype(BF16)
    o += MLA_Q_LORA
    ckv = _rms_norm(rows[:, o:o + MLA_KV_LORA], kvn_ref[...]).astype(BF16)
    o += MLA_KV_LORA
    k_rope = _rope_rows(rows[:, o:o + HEAD_PAD], ma, mb, mc, MLA_ROPE // 2)
    kc_ref[...] = (_dot(ckv, wuk_ref[...]) + jnp.concatenate([k_rope] * N_HEADS, axis=1)).astype(BF16)
    vct_ref[0] = _dot_nt(wuvt_ref[...], ckv).astype(BF16)


def _mixer_proj(x, mod, g_pre, wp, tabs):
    s, d = x.shape
    t = SEQ_TILE
    nb = s // t
    w = BRANCH_W
    hp = N_HEADS * HEAD_PAD
    row_spec = lambda n: pl.BlockSpec((t, n), lambda i: (i, 0))
    vt_spec = pl.BlockSpec((1, w, t), lambda i: (i, 0, 0))
    tab_spec = pl.BlockSpec((t, LANES), lambda i: (i, 0))
    out_shape = (
        jax.ShapeDtypeStruct((s, w), BF16), jax.ShapeDtypeStruct((s, w), BF16), jax.ShapeDtypeStruct((nb, w, t), BF16),
        jax.ShapeDtypeStruct((N_HEADS, nb, s), F32),
        jax.ShapeDtypeStruct((s, w), BF16), jax.ShapeDtypeStruct((s, w), BF16), jax.ShapeDtypeStruct((nb, w, t), BF16),
        jax.ShapeDtypeStruct((s, hp), BF16), jax.ShapeDtypeStruct((s, hp), BF16), jax.ShapeDtypeStruct((nb, w, t), BF16),
    )
    out_specs = (
        row_spec(w), row_spec(w), vt_spec,
        pl.BlockSpec((N_HEADS, nb, t), lambda i: (0, 0, i)),
        row_spec(w), row_spec(w), vt_spec,
        row_spec(hp), row_spec(hp), vt_spec,
    )
    in_specs = [
        pl.BlockSpec((t, d), lambda i: (i, 0)),
        _const_spec((3, d)), _const_spec((1, d)),
        _const_spec(wp["w_row"].shape), _const_spec(wp["w_uq"].shape),
        _const_spec(wp["w_uk"].shape), _const_spec(wp["w_uvt"].shape),
        _const_spec((1, MLA_Q_LORA)), _const_spec((1, MLA_KV_LORA)),
    ] + [tab_spec] * 6
    return pl.pallas_call(
        functools.partial(_proj_kernel, n_blocks=nb),
        grid=(nb,),
        in_specs=in_specs,
        out_specs=out_specs,
        out_shape=out_shape,
        scratch_shapes=[pltpu.VMEM((nb, w), F32)],
        compiler_params=_params(1),
        name="mixer_proj",
    )(x, mod, g_pre.reshape(1, d), wp["w_row"], wp["w_uq"], wp["w_uk"], wp["w_uvt"],
      wp["q_norm"], wp["kv_norm"], *tabs)


ATT_TILE = 512
SLAB_CHUNKS = ATT_TILE // SEQ_TILE
PAIR = 2
MOBA_GROUP = 8
MLA_GROUP = 4
ONES_ROWS = 16
LOG2E = 1.4426950408889634
SB_EXP2_IS_ZERO = -150.0


def _key_rows(c):
    return pl.ds(pl.multiple_of(c * SEQ_TILE, SEQ_TILE), SEQ_TILE)


def _slab_rows(j):
    return pl.ds(pl.multiple_of(j * ATT_TILE, ATT_TILE), ATT_TILE)


def _packed_heads(q_ref, k_ref, group):
    lane = lax.broadcasted_iota(jnp.int32, (1, LANES), 1)
    qs = []
    for e in range(group):
        q = q_ref[:, (e // PAIR) * LANES:(e // PAIR + 1) * LANES]
        qs.append(jnp.where((lane >= HEAD_DIM) if e % PAIR else (lane < HEAD_DIM), q, jnp.zeros_like(q)))
    return qs, lambda j, e: k_ref[_slab_rows(j), (e // PAIR) * LANES:(e // PAIR + 1) * LANES]


def _wide_heads(q_ref, k_ref, group):
    qs = [q_ref[:, e * LANES:(e + 1) * LANES] for e in range(group)]
    return qs, lambda j, e: k_ref[_slab_rows(j), e * LANES:(e + 1) * LANES]


def _value_t(vt_ref, c, e):
    return vt_ref[c, e * HEAD_DIM:(e + 1) * HEAD_DIM, :]


def _diag_positions(c):
    key = lax.broadcasted_iota(jnp.int32, (SEQ_TILE, ATT_TILE), 0) + c * SEQ_TILE
    qry = lax.broadcasted_iota(jnp.int32, (SEQ_TILE, ATT_TILE), 1)
    return key, qry


def _softmax_attention(q_pair, vt_ref, o_ref, s_refs, diag_mask, past_offset):
    qs, keys = q_pair
    group = len(qs)
    own = pl.program_id(1)
    ones = jnp.ones((ONES_ROWS, SEQ_TILE), BF16)

    def score(e, j, diagonal):
        col_max, offsets = None, []
        slab = _dot_nt(keys(j, e), qs[e])
        for c in range(SLAB_CHUNKS):
            cg = j * SLAB_CHUNKS + c
            s = slab[c * SEQ_TILE:(c + 1) * SEQ_TILE, :]
            off = None
            if diagonal:
                s = diag_mask(e, c, cg, s)
            elif past_offset is not None:
                off = past_offset(e, cg)
            s_refs[e][c * SEQ_TILE:(c + 1) * SEQ_TILE, :] = s
            cm = jnp.max(s, axis=0, keepdims=True)
            cm = cm if off is None else cm + off
            col_max = cm if col_max is None else jnp.maximum(col_max, cm)
            offsets.append(off)
        return col_max, offsets

    def update(e, j, carry, scored):
        m, acc = carry
        col_max, offsets = scored
        m_new = jnp.maximum(m, col_max)
        acc = jnp.exp2(m - m_new) * acc
        for c in range(SLAB_CHUNKS):
            shift = m_new if offsets[c] is None else m_new - offsets[c]
            p = jnp.exp2(s_refs[e][c * SEQ_TILE:(c + 1) * SEQ_TILE, :] - shift)
            v_ones = jnp.concatenate([_value_t(vt_ref, j * SLAB_CHUNKS + c, e), ones], axis=0)
            acc = acc + _dot(v_ones, p.astype(BF16))
        return m_new, acc

    t = ATT_TILE
    init = (jnp.full((1, t), MASKED, F32), jnp.zeros((HEAD_DIM + ONES_ROWS, t), F32))
    last = jnp.maximum(own - 1, 0)

    def ring(j, carries, scored, diagonal, j_ahead):
        carries = list(carries)
        for e in range(group):
            ahead = score(e + 1, j, diagonal) if e + 1 < group else score(0, j_ahead, False)
            carries[e] = update(e, j, carries[e], scored)
            scored = ahead
        return tuple(carries), scored

    def past(j, state):
        return ring(j, state[0], state[1], False, jnp.minimum(j + 1, last))

    state = ring(own, (init,) * group, score(0, own, True), True, 0)
    quads = lax.shift_right_logical(own, 2)

    def four(n, st):
        for u in range(4):
            st = past(4 * n + u, st)
        return st

    state = lax.fori_loop(0, quads, four, state)
    carries, _ = lax.fori_loop(4 * quads, own, past, state)
    out_t = jnp.concatenate([acc[:HEAD_DIM] / acc[HEAD_DIM:HEAD_DIM + 1] for _, acc in carries], axis=0)
    o_ref[...] = out_t.T.astype(o_ref.dtype)


def _moba_kernel(q_ref, k_ref, vt_ref, sel_ref, o_ref, *s_refs):
    def selected(e, cg):
        return sel_ref[e, pl.ds(cg, 1), :]

    def diag_mask(e, c, cg, s):
        key, qry = _diag_positions(c)
        own = lax.shift_right_logical(qry, MOBA_BLOCK.bit_length() - 1) == c
        return jnp.where(own, jnp.where(key <= qry, s, MASKED), s + selected(e, cg))

    _softmax_attention(_packed_heads(q_ref, k_ref, len(s_refs)), vt_ref, o_ref, s_refs, diag_mask, selected)


def _mla_kernel(q_ref, k_ref, vt_ref, o_ref, *s_refs):
    def diag_mask(e, c, cg, s):
        key, qry = _diag_positions(c)
        return jnp.where(key <= qry, s, MASKED)

    _softmax_attention(_wide_heads(q_ref, k_ref, len(s_refs)), vt_ref, o_ref, s_refs, diag_mask, None)


def _log2_1m_beta(z2):
    return -(jnp.maximum(z2, 0.0) + jnp.log2(1.0 + jnp.exp2(-jnp.abs(z2))))


def _sb_kernel(q_ref, k_ref, vt_ref, up_ref, o_ref, a_ref, hi_ref, lo_ref):
    i = pl.program_id(0)
    t = SEQ_TILE
    upper = up_ref[...]
    lane = lax.broadcasted_iota(jnp.int32, (1, LANES), 1)
    qs = []
    for h in range(N_HEADS):
        q = q_ref[:, (h // PAIR) * LANES:(h // PAIR + 1) * LANES]
        qs.append(jnp.where((lane >= HEAD_DIM) if h % PAIR else (lane < HEAD_DIM), q, jnp.zeros_like(q)))

    def score(h, j, keep):
        z = _dot_nt(k_ref[_key_rows(j), (h // PAIR) * LANES:(h // PAIR + 1) * LANES], qs[h])
        log1m = _log2_1m_beta(z)
        if keep is not None:
            log1m = jnp.where(keep, log1m, 0.0)
        hi = log1m.astype(BF16)
        a_ref[h] = z + log1m
        hi_ref[h] = hi
        lo_ref[h] = (log1m - hi.astype(F32)).astype(BF16)
        return jnp.sum(log1m, axis=0, keepdims=True)

    def weigh(h, j, carry, col_sum, keep):
        c, acc = carry
        suffix = _dot(upper, hi_ref[h]) + _dot(upper, lo_ref[h]) + c
        wgt = jnp.exp2(a_ref[h] + suffix)
        if keep is not None:
            wgt = jnp.where(keep, wgt, 0.0)
        return c + col_sum, acc + _dot(_value_t(vt_ref, j, h), wgt.astype(BF16))

    def chunk(j, carry, diagonal):
        keep = None
        if diagonal:
            keep = lax.broadcasted_iota(jnp.int32, (t, t), 0) < lax.broadcasted_iota(jnp.int32, (t, t), 1)
        out = []
        col_sum = score(0, j, keep)
        for h in range(N_HEADS):
            nxt = score(h + 1, j, keep) if h + 1 < N_HEADS else None
            out.append(weigh(h, j, carry[h], col_sum, keep))
            col_sum = nxt
        return tuple(out)

    init = ((jnp.zeros((1, t), F32), jnp.zeros((HEAD_DIM, t), F32)),) * N_HEADS
    carry = chunk(i, init, True)

    def alive(state):
        n, carry = state
        c_max = carry[0][0]
        for h in range(1, N_HEADS):
            c_max = jnp.maximum(c_max, carry[h][0])
        return jnp.logical_and(n < i, jnp.max(c_max) > SB_EXP2_IS_ZERO)

    _, carry = lax.while_loop(alive, lambda st: (st[0] + 1, chunk(i - 1 - st[0], st[1], False)), (0, carry))
    o_ref[...] = jnp.concatenate([carry[h][1] for h in range(N_HEADS)], axis=0).T.astype(o_ref.dtype)


def _stick_breaking(q, k, vt, upper):
    s, w = q.shape
    t = SEQ_TILE
    return pl.pallas_call(
        _sb_kernel,
        grid=(s // t,),
        in_specs=[pl.BlockSpec((t, w), lambda i: (i, 0)), _const_spec((s, w)), _const_spec((s // t, w, t)),
                  _const_spec((t, t))],
        out_specs=pl.BlockSpec((t, w), lambda i: (i, 0)),
        out_shape=jax.ShapeDtypeStruct((s, w), BF16),
        scratch_shapes=[pltpu.VMEM((N_HEADS, t, t), F32), pltpu.VMEM((N_HEADS, t, t), BF16),
                        pltpu.VMEM((N_HEADS, t, t), BF16)],
        compiler_params=_params(1),
        name="stick_breaking_attention",
    )(q, k, vt, upper)


def _attention(kernel, q, k, vt, extra, extra_specs, group, group_lanes, name):
    s = q.shape[0]
    nb = s // SEQ_TILE
    rows = group * HEAD_DIM
    return pl.pallas_call(
        kernel,
        grid=(N_HEADS // group, s // ATT_TILE),
        in_specs=[
            pl.BlockSpec((ATT_TILE, group_lanes), lambda g, i: (i, g)),
            pl.BlockSpec((s, group_lanes), lambda g, i: (0, g), pipeline_mode=pl.Buffered(1)),
            pl.BlockSpec((nb, rows, SEQ_TILE), lambda g, i: (0, g, 0), pipeline_mode=pl.Buffered(1)),
        ] + extra_specs,
        out_specs=pl.BlockSpec((ATT_TILE, rows), lambda g, i: (i, g)),
        out_shape=jax.ShapeDtypeStruct((s, N_HEADS * HEAD_DIM), BF16),
        scratch_shapes=[pltpu.VMEM((ATT_TILE, ATT_TILE), F32)] * group,
        compiler_params=_params(2),
        name=name,
    )(q, k, vt, *extra)


def _merge_kernel(x_ref, ya_ref, yb_ref, yc_ref, wg_ref, bg_ref, wb_ref, wo_ref, mod_ref, gpre_ref, gpost_ref, o_ref):
    d = D_MODEL
    x = x_ref[...]
    shift, scale, gate = mod_ref[0:1, :], mod_ref[1:2, :], mod_ref[2:3, :]
    hn = (_rms_norm(x, gpre_ref[...]) * (1.0 + scale) + shift).astype(BF16)
    g = jax.nn.sigmoid(_dot(hn, wg_ref[...]) + bg_ref[...])
    merged = None
    for n, y_ref in enumerate((ya_ref, yb_ref, yc_ref)):
        part = _dot(y_ref[...], wb_ref[n]) * g[:, n * d:(n + 1) * d]
        merged = part if merged is None else merged + part
    out = _dot(merged.astype(BF16), wo_ref[...])
    o_ref[...] = x + (1.0 + gate) * _rms_norm(out, gpost_ref[...])


def _merge(x, ya, yb, yc, wp, mod, g_pre, g_post):
    s, d = x.shape
    t = ATT_TILE
    row_spec = lambda n: pl.BlockSpec((t, n), lambda i: (i, 0))
    return pl.pallas_call(
        _merge_kernel,
        grid=(s // t,),
        in_specs=[
            row_spec(d), row_spec(BRANCH_W), row_spec(BRANCH_W), row_spec(BRANCH_W),
            _const_spec(wp["w_g"].shape), _const_spec((1, N_BRANCH * d)), _const_spec(wp["w_b"].shape),
            _const_spec(wp["w_o"].shape), _const_spec((3, d)), _const_spec((1, d)), _const_spec((1, d)),
        ],
        out_specs=row_spec(d),
        out_shape=jax.ShapeDtypeStruct((s, d), F32),
        compiler_params=_params(1),
        name="mixer_merge",
    )(x, ya, yb, yc, wp["w_g"], wp["b_gate"], wp["w_b"], wp["w_o"], mod, g_pre.reshape(1, d), g_post.reshape(1, d))


def _rope_tables(n_rot, seq, first_lane, period):
    half = n_rot // 2
    assert first_lane % half == 0 and period % half == 0
    inv = jnp.power(jnp.float32(ROPE_THETA), -jnp.arange(0, n_rot, 2, dtype=F32) / n_rot)
    ang = jnp.arange(seq, dtype=F32)[:, None] * inv[None, :]
    cos, sin = jnp.tile(jnp.cos(ang), (1, LANES // half)), jnp.tile(jnp.sin(ang), (1, LANES // half))
    lane = jnp.arange(LANES) % period - first_lane
    lo = ((lane >= 0) & (lane < half))[None, :]
    hi = ((lane >= half) & (lane < n_rot))[None, :]
    return jnp.where(lo | hi, cos, 1.0), jnp.where(lo, -sin, 0.0), jnp.where(hi, sin, 0.0)


def _pad_heads(w, n_heads, width):
    k = w.shape[0]
    w = w.reshape(k, n_heads, width)
    return jnp.pad(w, ((0, 0), (0, 0), (0, HEAD_PAD - width))).reshape(k, n_heads * HEAD_PAD)


def _mixer_weights(w_in, b_gate, q_norm, w_uq, kv_norm, w_ukv, w_branch, w_out):
    w = BRANCH_W
    d = D_MODEL
    qa, ka, va, qb, kb, vb = (w_in[:, n * w:(n + 1) * w] for n in range(6))
    o = 6 * w
    w_cq = w_in[:, o:o + MLA_Q_LORA]
    o += MLA_Q_LORA
    w_ckv = w_in[:, o:o + MLA_KV_LORA]
    o += MLA_KV_LORA
    w_kr = jnp.pad(w_in[:, o:o + MLA_ROPE], ((0, 0), (MLA_NOPE, HEAD_PAD - MLA_QK)))
    o += MLA_ROPE
    w_gates = w_in[:, o:]
    ukv = w_ukv.reshape(MLA_KV_LORA, N_HEADS, MLA_NOPE + MLA_V)
    return {
        "w_row": jnp.concatenate([qa, ka, qb, kb, w_cq, w_ckv, w_kr, va, vb], axis=1),
        "w_g": w_gates,
        "w_uq": _pad_heads(w_uq, N_HEADS, MLA_QK).astype(BF16),
        "w_uk": _pad_heads(ukv[:, :, :MLA_NOPE].reshape(MLA_KV_LORA, -1), N_HEADS, MLA_NOPE).astype(BF16),
        "w_uvt": ukv[:, :, MLA_NOPE:].reshape(MLA_KV_LORA, -1).T.astype(BF16),
        "q_norm": q_norm.reshape(1, -1),
        "kv_norm": kv_norm.reshape(1, -1),
        "b_gate": b_gate.reshape(1, N_BRANCH * d),
        "w_b": w_branch,
        "w_o": w_out,
    }


def kernel(x, c, ada_w, ada_b, norm_pre, norm_post, ffn_w_gate, ffn_w_up, ffn_w_down, mix_w_in, mix_b_gate,
           mla_q_norm, mla_w_uq, mla_kv_norm, mla_w_ukv, mix_w_branch, mix_w_out):
    bsz, s, d = x.shape
    assert bsz == 1 and d == D_MODEL and s % ATT_TILE == 0 and s % min(FFN_TILE, s) == 0
    depth = ada_w.shape[0]
    t = SEQ_TILE
    mod = _modulation(c, ada_w, ada_b)
    tabs = _rope_tables(PARTIAL_ROT, s, 0, HEAD_DIM) + _rope_tables(MLA_ROPE, s, MLA_NOPE, HEAD_PAD)
    upper = (jnp.arange(t)[None, :] > jnp.arange(t)[:, None]).astype(BF16)
    xs = x.reshape(s, d)
    wg, wu, wd = ffn_w_gate.astype(BF16), ffn_w_up.astype(BF16), ffn_w_down.astype(BF16)
    w_in, w_branch, w_out = mix_w_in.astype(BF16), mix_w_branch.astype(BF16), mix_w_out.astype(BF16)
    for l in range(depth):
        xs = _ffn(xs, mod[l, 0], norm_pre[l, 0], norm_post[l, 0], wg, wu, wd, l, 0, 0.5)
        wp = _mixer_weights(w_in[l], mix_b_gate[l], mla_q_norm[l], mla_w_uq[l], mla_kv_norm[l], mla_w_ukv[l],
                            w_branch[l], w_out[l])
        qa, ka, vat, sel, qb, kb, vbt, qc, kc, vct = _mixer_proj(xs, mod[l, 1], norm_pre[l, 1], wp, tabs)
        ya = _attention(_moba_kernel, qa, ka, vat, (sel,),
                        [pl.BlockSpec((MOBA_GROUP, s // t, ATT_TILE), lambda g, i: (g, 0, i))], MOBA_GROUP,
                        MOBA_GROUP // PAIR * LANES, "moba_attention")
        yb = _stick_breaking(qb, kb, vbt, upper)
        yc = _attention(_mla_kernel, qc, kc, vct, (), [], MLA_GROUP, MLA_GROUP * HEAD_PAD, "mla_attention")
        xs = _merge(xs, ya, yb, yc, wp, mod[l, 1], norm_pre[l, 1], norm_post[l, 1])
        xs = _ffn(xs, mod[l, 2], norm_pre[l, 2], norm_post[l, 2], wg, wu, wd, l, 1, 0.5)
    return xs.reshape(bsz, s, d)
```

```python
import functools

import jax
import jax.numpy as jnp
from jax import lax
from jax.experimental import pallas as pl
from jax.experimental.pallas import tpu as pltpu

F32 = jnp.float32
BF16 = jnp.bfloat16

D_MODEL = 1024
N_HEADS = 8
HEAD_DIM = 64
MOBA_BLOCK = 256
MOBA_TOPK = 3
MLA_Q_LORA = 256
MLA_KV_LORA = 128
MLA_NOPE = 64
MLA_ROPE = 32
MLA_V = 64
MLA_QK = MLA_NOPE + MLA_ROPE
ROPE_THETA = 500000.0
PARTIAL_ROT = HEAD_DIM // 4
D_FF = 2816
N_BRANCH = 3
BRANCH_W = N_HEADS * HEAD_DIM
N_SUB = 3
EPS = 1e-6

LANES = 128
V7X_VMEM_LIMIT = 56 * 1024 * 1024

SEQ_TILE = MOBA_BLOCK
FFN_TILE = 512
MOD_COLUMN_BLOCKS = 8
MASKED = -1e30
HEAD_PAD = LANES


def _dot(a, b):
    return jnp.dot(a, b, preferred_element_type=F32)


def _dot_nt(a, b, precision=None):
    return lax.dot_general(a, b, (((1,), (1,)), ((), ())), precision=precision, preferred_element_type=F32)


def _rms_norm(x, g):
    return x * lax.rsqrt(jnp.mean(x * x, axis=-1, keepdims=True) + EPS) * g


def _const_spec(shape):
    nd = len(shape)
    return pl.BlockSpec(shape, lambda *_: (0,) * nd, pipeline_mode=pl.Buffered(1))


def _params(n_grid):
    return pltpu.CompilerParams(dimension_semantics=("arbitrary",) * n_grid, vmem_limit_bytes=V7X_VMEM_LIMIT)


def _mod_kernel(c_ref, w_ref, b_ref, o_ref):
    c = c_ref[...]
    ca = c * jax.nn.sigmoid(c)
    o_ref[0] = jnp.sum(ca * w_ref[0], axis=0, keepdims=True) + b_ref[0]


def _modulation(c, ada_w, ada_b):
    depth, d, n = ada_w.shape
    tn = n // MOD_COLUMN_BLOCKS
    out = pl.pallas_call(
        _mod_kernel,
        grid=(depth, n // tn),
        in_specs=[
            pl.BlockSpec((d, 1), lambda l, j: (0, 0)),
            pl.BlockSpec((1, d, tn), lambda l, j: (l, 0, j)),
            pl.BlockSpec((1, 1, tn), lambda l, j: (l, 0, j)),
        ],
        out_specs=pl.BlockSpec((1, 1, tn), lambda l, j: (l, 0, j)),
        out_shape=jax.ShapeDtypeStruct((depth, 1, n), F32),
        compiler_params=_params(2),
        name="adaln_mod",
    )(c.reshape(d, 1), ada_w, ada_b.reshape(depth, 1, n))
    return out.reshape(depth, N_SUB, 3, d)


def _ffn_kernel(x_ref, mod_ref, gpre_ref, gpost_ref, wg_ref, wu_ref, wd_ref, o_ref, *, res_w):
    x = x_ref[...]
    shift, scale, gate = mod_ref[0:1, :], mod_ref[1:2, :], mod_ref[2:3, :]
    h = (_rms_norm(x, gpre_ref[...]) * (1.0 + scale) + shift).astype(BF16)
    g = _dot(h, wg_ref[...])
    u = _dot(h, wu_ref[...])
    a = (g * jax.nn.sigmoid(g) * u).astype(BF16)
    y = _dot(a, wd_ref[...])
    o_ref[...] = x + (res_w * (1.0 + gate)) * _rms_norm(y, gpost_ref[...])


def _ffn(x, mod, g_pre, g_post, wg, wu, wd, layer, half, res_w):
    s, d = x.shape
    tm = min(FFN_TILE, s)
    dff = wg.shape[-1]
    weight_spec = lambda shape: pl.BlockSpec((None, None) + shape, lambda i: (layer, half, 0, 0),
                                             pipeline_mode=pl.Buffered(1))
    return pl.pallas_call(
        functools.partial(_ffn_kernel, res_w=res_w),
        grid=(s // tm,),
        in_specs=[
            pl.BlockSpec((tm, d), lambda i: (i, 0)),
            _const_spec((3, d)),
            _const_spec((1, d)),
            _const_spec((1, d)),
            weight_spec((d, dff)),
            weight_spec((d, dff)),
            weight_spec((dff, d)),
        ],
        out_specs=pl.BlockSpec((tm, d), lambda i: (i, 0)),
        out_shape=jax.ShapeDtypeStruct((s, d), F32),
        compiler_params=_params(1),
        name="macaron_ffn",
    )(x, mod, g_pre.reshape(1, d), g_post.reshape(1, d), wg, wu, wd)


def _rope_rows(x, a, b, c, half):
    outs = []
    for g in range(x.shape[1] // LANES):
        xg = x[:, g * LANES:(g + 1) * LANES]
        outs.append(xg * a + pltpu.roll(xg, LANES - half, 1) * b + pltpu.roll(xg, half, 1) * c)
    return jnp.concatenate(outs, axis=1) if len(outs) > 1 else outs[0]


def _proj_kernel(x_ref, mod_ref, gpre_ref, wrow_ref, wuq_ref, wuk_ref, wuvt_ref, qn_ref, kvn_ref,
                 pa_ref, pb_ref, pc_ref, ma_ref, mb_ref, mc_ref,
                 qa_ref, ka_ref, vat_ref, sel_ref, qb_ref, kb_ref, vbt_ref, qc_ref, kc_ref, vct_ref,
                 kmean_ref, *, n_blocks):
    i = pl.program_id(0)
    w = BRANCH_W
    x = x_ref[...]
    shift, scale = mod_ref[0:1, :], mod_ref[1:2, :]
    hn = (_rms_norm(x, gpre_ref[...]) * (1.0 + scale) + shift).astype(BF16)
    rows = _dot(hn, wrow_ref[...])
    cols_t = rows[:, rows.shape[1] - 2 * w:].T

    pa, pb, pc = pa_ref[...], pb_ref[...], pc_ref[...]
    q_a = _rope_rows(rows[:, 0:w], pa, pb, pc, PARTIAL_ROT // 2)
    k_a = _rope_rows(rows[:, w:2 * w], pa, pb, pc, PARTIAL_ROT // 2)
    qa_ref[...] = (q_a * (HEAD_DIM ** -0.5 * LOG2E)).astype(BF16)
    ka_ref[...] = k_a.astype(BF16)
    vat_ref[0] = cols_t[0:w, :].astype(BF16)

    @pl.when(i == 0)
    def _():
        kmean_ref[...] = jnp.zeros_like(kmean_ref)

    kmean_ref[pl.ds(i, 1), :] = jnp.mean(k_a, axis=0, keepdims=True)
    kmean = kmean_ref[...]
    t = x.shape[0]
    lane = lax.broadcasted_iota(jnp.int32, (1, LANES), 1)
    blk = lax.broadcasted_iota(jnp.int32, (n_blocks, t), 0)
    past = blk < i
    blk_f = blk.astype(F32)
    for h in range(N_HEADS):
        p, e = h // 2, h % 2
        if e == 0:
            q_pair = q_a[:, p * LANES:(p + 1) * LANES]
            qm = jnp.concatenate([jnp.where(lane < HEAD_DIM, q_pair, 0.0), jnp.where(lane >= HEAD_DIM, q_pair, 0.0)],
                                 axis=0)
            gates = _dot_nt(kmean[:, p * LANES:(p + 1) * LANES], qm, precision=lax.Precision.HIGHEST)
        gate = jnp.where(past, gates[:, e * t:(e + 1) * t], -jnp.inf)
        chosen = jnp.zeros(gate.shape, F32)
        for _ in range(min(MOBA_TOPK, n_blocks)):
            top = jnp.max(gate, axis=0, keepdims=True)
            first = jnp.min(jnp.where(gate == top, blk_f, float(n_blocks)), axis=0, keepdims=True)
            pick = blk_f == first
            chosen = jnp.where(pick, 1.0, chosen)
            gate = jnp.where(pick, -jnp.inf, gate)
        sel_ref[h] = jnp.where(jnp.logical_and(chosen > 0.0, past), 0.0, MASKED)

    qb_ref[...] = (rows[:, 2 * w:3 * w] * (HEAD_DIM ** -0.5 * LOG2E)).astype(BF16)
    kb_ref[...] = rows[:, 3 * w:4 * w].astype(BF16)
    vbt_ref[0] = cols_t[w:2 * w, :].astype(BF16)

    o = 4 * w
    ma, mb, mc = ma_ref[...], mb_ref[...], mc_ref[...]
    cq = _rms_norm(rows[:, o:o + MLA_Q_LORA], qn_ref[...]).astype(BF16)
    q_c = _rope_rows(_dot(cq, wuq_ref[...]), ma, mb, mc, MLA_ROPE // 2)
    qc_ref[...] = (q_c * (MLA_QK ** -0.5 * LOG2E)).astype(BF16)
    o += MLA_Q_LORA
    ckv = _rms_norm(rows[:, o:o + MLA_KV_LORA], kvn_ref[...]).astype(BF16)
    o += MLA_KV_LORA
    k_rope = _rope_rows(rows[:, o:o + HEAD_PAD], ma, mb, mc, MLA_ROPE // 2)
    kc_ref[...] = (_dot(ckv, wuk_ref[...]) + jnp.concatenate([k_rope] * N_HEADS, axis=1)).astype(BF16)
    vct_ref[0] = _dot_nt(wuvt_ref[...], ckv).astype(BF16)


def _mixer_proj(x, mod, g_pre, wp, tabs):
    s, d = x.shape
    t = SEQ_TILE
    nb = s // t
    w = BRANCH_W
    hp = N_HEADS * HEAD_PAD
    row_spec = lambda n: pl.BlockSpec((t, n), lambda i: (i, 0))
    vt_spec = pl.BlockSpec((1, w, t), lambda i: (i, 0, 0))
    tab_spec = pl.BlockSpec((t, LANES), lambda i: (i, 0))
    out_shape = (
        jax.ShapeDtypeStruct((s, w), BF16), jax.ShapeDtypeStruct((s, w), BF16), jax.ShapeDtypeStruct((nb, w, t), BF16),
        jax.ShapeDtypeStruct((N_HEADS, nb, s), F32),
        jax.ShapeDtypeStruct((s, w), BF16), jax.ShapeDtypeStruct((s, w), BF16), jax.ShapeDtypeStruct((nb, w, t), BF16),
        jax.ShapeDtypeStruct((s, hp), BF16), jax.ShapeDtypeStruct((s, hp), BF16), jax.ShapeDtypeStruct((nb, w, t), BF16),
    )
    out_specs = (
        row_spec(w), row_spec(w), vt_spec,
        pl.BlockSpec((N_HEADS, nb, t), lambda i: (0, 0, i)),
        row_spec(w), row_spec(w), vt_spec,
        row_spec(hp), row_spec(hp), vt_spec,
    )
    in_specs = [
        pl.BlockSpec((t, d), lambda i: (i, 0)),
        _const_spec((3, d)), _const_spec((1, d)),
        _const_spec(wp["w_row"].shape), _const_spec(wp["w_uq"].shape),
        _const_spec(wp["w_uk"].shape), _const_spec(wp["w_uvt"].shape),
        _const_spec((1, MLA_Q_LORA)), _const_spec((1, MLA_KV_LORA)),
    ] + [tab_spec] * 6
    return pl.pallas_call(
        functools.partial(_proj_kernel, n_blocks=nb),
        grid=(nb,),
        in_specs=in_specs,
        out_specs=out_specs,
        out_shape=out_shape,
        scratch_shapes=[pltpu.VMEM((nb, w), F32)],
        compiler_params=_params(1),
        name="mixer_proj",
    )(x, mod, g_pre.reshape(1, d), wp["w_row"], wp["w_uq"], wp["w_uk"], wp["w_uvt"],
      wp["q_norm"], wp["kv_norm"], *tabs)


ATT_TILE = 512
SLAB_CHUNKS = ATT_TILE // SEQ_TILE
PAIR = 2
MOBA_GROUP = 8
MLA_GROUP = 4
ONES_ROWS = 16
LOG2E = 1.4426950408889634
SB_EXP2_IS_ZERO = -150.0


def _key_rows(c):
    return pl.ds(pl.multiple_of(c * SEQ_TILE, SEQ_TILE), SEQ_TILE)


def _slab_rows(j):
    return pl.ds(pl.multiple_of(j * ATT_TILE, ATT_TILE), ATT_TILE)


def _packed_heads(q_ref, k_ref, group):
    lane = lax.broadcasted_iota(jnp.int32, (1, LANES), 1)
    qs = []
    for e in range(group):
        q = q_ref[:, (e // PAIR) * LANES:(e // PAIR + 1) * LANES]
        qs.append(jnp.where((lane >= HEAD_DIM) if e % PAIR else (lane < HEAD_DIM), q, jnp.zeros_like(q)))
    return qs, lambda j, e: k_ref[_slab_rows(j), (e // PAIR) * LANES:(e // PAIR + 1) * LANES]


def _wide_heads(q_ref, k_ref, group):
    qs = [q_ref[:, e * LANES:(e + 1) * LANES] for e in range(group)]
    return qs, lambda j, e: k_ref[_slab_rows(j), e * LANES:(e + 1) * LANES]


def _value_t(vt_ref, c, e):
    return vt_ref[c, e * HEAD_DIM:(e + 1) * HEAD_DIM, :]


def _diag_positions(c):
    key = lax.broadcasted_iota(jnp.int32, (SEQ_TILE, ATT_TILE), 0) + c * SEQ_TILE
    qry = lax.broadcasted_iota(jnp.int32, (SEQ_TILE, ATT_TILE), 1)
    return key, qry


def _softmax_attention(q_pair, vt_ref, o_ref, s_refs, diag_mask, past_offset):
    qs, keys = q_pair
    group = len(qs)
    own = pl.program_id(1)
    ones = jnp.ones((ONES_ROWS, SEQ_TILE), BF16)

    def score(e, j, diagonal):
        col_max, offsets = None, []
        slab = _dot_nt(keys(j, e), qs[e])
        for c in range(SLAB_CHUNKS):
            cg = j * SLAB_CHUNKS + c
            s = slab[c * SEQ_TILE:(c + 1) * SEQ_TILE, :]
            off = None
            if diagonal:
                s = diag_mask(e, c, cg, s)
            elif past_offset is not None:
                off = past_offset(e, cg)
            s_refs[e][c * SEQ_TILE:(c + 1) * SEQ_TILE, :] = s
            cm = jnp.max(s, axis=0, keepdims=True)
            cm = cm if off is None else cm + off
            col_max = cm if col_max is None else jnp.maximum(col_max, cm)
            offsets.append(off)
        return col_max, offsets

    def update(e, j, carry, scored):
        m, acc = carry
        col_max, offsets = scored
        m_new = jnp.maximum(m, col_max)
        acc = jnp.exp2(m - m_new) * acc
        for c in range(SLAB_CHUNKS):
            shift = m_new if offsets[c] is None else m_new - offsets[c]
            p = jnp.exp2(s_refs[e][c * SEQ_TILE:(c + 1) * SEQ_TILE, :] - shift)
            v_ones = jnp.concatenate([_value_t(vt_ref, j * SLAB_CHUNKS + c, e), ones], axis=0)
            acc = acc + _dot(v_ones, p.astype(BF16))
        return m_new, acc

    t = ATT_TILE
    init = (jnp.full((1, t), MASKED, F32), jnp.zeros((HEAD_DIM + ONES_ROWS, t), F32))
    last = jnp.maximum(own - 1, 0)

    def ring(j, carries, scored, diagonal, j_ahead):
        carries, scored = list(carries), list(scored)
        for e in range(group):
            ahead = score(e + 2, j, diagonal) if e + 2 < group else score(e + 2 - group, j_ahead, False)
            carries[e] = update(e, j, carries[e], scored[0])
            scored = [scored[1], ahead]
        return tuple(carries), tuple(scored)

    def past(j, state):
        return ring(j, state[0], state[1], False, jnp.minimum(j + 1, last))

    state = ring(own, (init,) * group, (score(0, own, True), score(1, own, True)), True, 0)
    quads = lax.shift_right_logical(own, 2)

    def four(n, st):
        for u in range(4):
            st = past(4 * n + u, st)
        return st

    state = lax.fori_loop(0, quads, four, state)
    carries, _ = lax.fori_loop(4 * quads, own, past, state)
    out_t = jnp.concatenate([acc[:HEAD_DIM] / acc[HEAD_DIM:HEAD_DIM + 1] for _, acc in carries], axis=0)
    o_ref[...] = out_t.T.astype(o_ref.dtype)


def _moba_kernel(q_ref, k_ref, vt_ref, sel_ref, o_ref, *s_refs):
    def selected(e, cg):
        return sel_ref[e, pl.ds(cg, 1), :]

    def diag_mask(e, c, cg, s):
        key, qry = _diag_positions(c)
        own = lax.shift_right_logical(qry, MOBA_BLOCK.bit_length() - 1) == c
        return jnp.where(own, jnp.where(key <= qry, s, MASKED), s + selected(e, cg))

    _softmax_attention(_packed_heads(q_ref, k_ref, len(s_refs)), vt_ref, o_ref, s_refs, diag_mask, selected)


def _mla_kernel(q_ref, k_ref, vt_ref, o_ref, *s_refs):
    def diag_mask(e, c, cg, s):
        key, qry = _diag_positions(c)
        return jnp.where(key <= qry, s, MASKED)

    _softmax_attention(_wide_heads(q_ref, k_ref, len(s_refs)), vt_ref, o_ref, s_refs, diag_mask, None)


def _log2_1m_beta(z2):
    return -(jnp.maximum(z2, 0.0) + jnp.log2(1.0 + jnp.exp2(-jnp.abs(z2))))


def _sb_kernel(q_ref, k_ref, vt_ref, up_ref, o_ref, a_ref, hi_ref, lo_ref):
    i = pl.program_id(0)
    t = SEQ_TILE
    upper = up_ref[...]
    lane = lax.broadcasted_iota(jnp.int32, (1, LANES), 1)
    qs = []
    for h in range(N_HEADS):
        q = q_ref[:, (h // PAIR) * LANES:(h // PAIR + 1) * LANES]
        qs.append(jnp.where((lane >= HEAD_DIM) if h % PAIR else (lane < HEAD_DIM), q, jnp.zeros_like(q)))

    def score(h, j, keep):
        z = _dot_nt(k_ref[_key_rows(j), (h // PAIR) * LANES:(h // PAIR + 1) * LANES], qs[h])
        log1m = _log2_1m_beta(z)
        if keep is not None:
            log1m = jnp.where(keep, log1m, 0.0)
        hi = log1m.astype(BF16)
        a_ref[h] = z + log1m
        hi_ref[h] = hi
        lo_ref[h] = (log1m - hi.astype(F32)).astype(BF16)
        return jnp.sum(log1m, axis=0, keepdims=True)

    def weigh(h, j, carry, col_sum, keep):
        c, acc = carry
        suffix = _dot(upper, hi_ref[h]) + _dot(upper, lo_ref[h]) + c
        wgt = jnp.exp2(a_ref[h] + suffix)
        if keep is not None:
            wgt = jnp.where(keep, wgt, 0.0)
        return c + col_sum, acc + _dot(_value_t(vt_ref, j, h), wgt.astype(BF16))

    def chunk(j, carry, diagonal):
        keep = None
        if diagonal:
            keep = lax.broadcasted_iota(jnp.int32, (t, t), 0) < lax.broadcasted_iota(jnp.int32, (t, t), 1)
        out = []
        col_sum = score(0, j, keep)
        for h in range(N_HEADS):
            nxt = score(h + 1, j, keep) if h + 1 < N_HEADS else None
            out.append(weigh(h, j, carry[h], col_sum, keep))
            col_sum = nxt
        return tuple(out)

    init = ((jnp.zeros((1, t), F32), jnp.zeros((HEAD_DIM, t), F32)),) * N_HEADS
    carry = chunk(i, init, True)

    def alive(state):
        n, carry = state
        c_max = carry[0][0]
        for h in range(1, N_HEADS):
            c_max = jnp.maximum(c_max, carry[h][0])
        return jnp.logical_and(n < i, jnp.max(c_max) > SB_EXP2_IS_ZERO)

    _, carry = lax.while_loop(alive, lambda st: (st[0] + 1, chunk(i - 1 - st[0], st[1], False)), (0, carry))
    o_ref[...] = jnp.concatenate([carry[h][1] for h in range(N_HEADS)], axis=0).T.astype(o_ref.dtype)


def _stick_breaking(q, k, vt, upper):
    s, w = q.shape
    t = SEQ_TILE
    return pl.pallas_call(
        _sb_kernel,
        grid=(s // t,),
        in_specs=[pl.BlockSpec((t, w), lambda i: (i, 0)), _const_spec((s, w)), _const_spec((s // t, w, t)),
                  _const_spec((t, t))],
        out_specs=pl.BlockSpec((t, w), lambda i: (i, 0)),
        out_shape=jax.ShapeDtypeStruct((s, w), BF16),
        scratch_shapes=[pltpu.VMEM((N_HEADS, t, t), F32), pltpu.VMEM((N_HEADS, t, t), BF16),
                        pltpu.VMEM((N_HEADS, t, t), BF16)],
        compiler_params=_params(1),
        name="stick_breaking_attention",
    )(q, k, vt, upper)


def _attention(kernel, q, k, vt, extra, extra_specs, group, group_lanes, name):
    s = q.shape[0]
    nb = s // SEQ_TILE
    rows = group * HEAD_DIM
    return pl.pallas_call(
        kernel,
        grid=(N_HEADS // group, s // ATT_TILE),
        in_specs=[
            pl.BlockSpec((ATT_TILE, group_lanes), lambda g, i: (i, g)),
            pl.BlockSpec((s, group_lanes), lambda g, i: (0, g), pipeline_mode=pl.Buffered(1)),
            pl.BlockSpec((nb, rows, SEQ_TILE), lambda g, i: (0, g, 0), pipeline_mode=pl.Buffered(1)),
        ] + extra_specs,
        out_specs=pl.BlockSpec((ATT_TILE, rows), lambda g, i: (i, g)),
        out_shape=jax.ShapeDtypeStruct((s, N_HEADS * HEAD_DIM), BF16),
        scratch_shapes=[pltpu.VMEM((ATT_TILE, ATT_TILE), F32)] * group,
        compiler_params=_params(2),
        name=name,
    )(q, k, vt, *extra)


def _merge_kernel(x_ref, ya_ref, yb_ref, yc_ref, wg_ref, bg_ref, wb_ref, wo_ref, mod_ref, gpre_ref, gpost_ref, o_ref):
    d = D_MODEL
    x = x_ref[...]
    shift, scale, gate = mod_ref[0:1, :], mod_ref[1:2, :], mod_ref[2:3, :]
    hn = (_rms_norm(x, gpre_ref[...]) * (1.0 + scale) + shift).astype(BF16)
    g = jax.nn.sigmoid(_dot(hn, wg_ref[...]) + bg_ref[...])
    merged = None
    for n, y_ref in enumerate((ya_ref, yb_ref, yc_ref)):
        part = _dot(y_ref[...], wb_ref[n]) * g[:, n * d:(n + 1) * d]
        merged = part if merged is None else merged + part
    out = _dot(merged.astype(BF16), wo_ref[...])
    o_ref[...] = x + (1.0 + gate) * _rms_norm(out, gpost_ref[...])


def _merge(x, ya, yb, yc, wp, mod, g_pre, g_post):
    s, d = x.shape
    t = ATT_TILE
    row_spec = lambda n: pl.BlockSpec((t, n), lambda i: (i, 0))
    return pl.pallas_call(
        _merge_kernel,
        grid=(s // t,),
        in_specs=[
            row_spec(d), row_spec(BRANCH_W), row_spec(BRANCH_W), row_spec(BRANCH_W),
            _const_spec(wp["w_g"].shape), _const_spec((1, N_BRANCH * d)), _const_spec(wp["w_b"].shape),
            _const_spec(wp["w_o"].shape), _const_spec((3, d)), _const_spec((1, d)), _const_spec((1, d)),
        ],
        out_specs=row_spec(d),
        out_shape=jax.ShapeDtypeStruct((s, d), F32),
        compiler_params=_params(1),
        name="mixer_merge",
    )(x, ya, yb, yc, wp["w_g"], wp["b_gate"], wp["w_b"], wp["w_o"], mod, g_pre.reshape(1, d), g_post.reshape(1, d))


def _rope_tables(n_rot, seq, first_lane, period):
    half = n_rot // 2
    assert first_lane % half == 0 and period % half == 0
    inv = jnp.power(jnp.float32(ROPE_THETA), -jnp.arange(0, n_rot, 2, dtype=F32) / n_rot)
    ang = jnp.arange(seq, dtype=F32)[:, None] * inv[None, :]
    cos, sin = jnp.tile(jnp.cos(ang), (1, LANES // half)), jnp.tile(jnp.sin(ang), (1, LANES // half))
    lane = jnp.arange(LANES) % period - first_lane
    lo = ((lane >= 0) & (lane < half))[None, :]
    hi = ((lane >= half) & (lane < n_rot))[None, :]
    return jnp.where(lo | hi, cos, 1.0), jnp.where(lo, -sin, 0.0), jnp.where(hi, sin, 0.0)


def _pad_heads(w, n_heads, width):
    k = w.shape[0]
    w = w.reshape(k, n_heads, width)
    return jnp.pad(w, ((0, 0), (0, 0), (0, HEAD_PAD - width))).reshape(k, n_heads * HEAD_PAD)


def _mixer_weights(w_in, b_gate, q_norm, w_uq, kv_norm, w_ukv, w_branch, w_out):
    w = BRANCH_W
    d = D_MODEL
    qa, ka, va, qb, kb, vb = (w_in[:, n * w:(n + 1) * w] for n in range(6))
    o = 6 * w
    w_cq = w_in[:, o:o + MLA_Q_LORA]
    o += MLA_Q_LORA
    w_ckv = w_in[:, o:o + MLA_KV_LORA]
    o += MLA_KV_LORA
    w_kr = jnp.pad(w_in[:, o:o + MLA_ROPE], ((0, 0), (MLA_NOPE, HEAD_PAD - MLA_QK)))
    o += MLA_ROPE
    w_gates = w_in[:, o:]
    ukv = w_ukv.reshape(MLA_KV_LORA, N_HEADS, MLA_NOPE + MLA_V)
    return {
        "w_row": jnp.concatenate([qa, ka, qb, kb, w_cq, w_ckv, w_kr, va, vb], axis=1),
        "w_g": w_gates,
        "w_uq": _pad_heads(w_uq, N_HEADS, MLA_QK).astype(BF16),
        "w_uk": _pad_heads(ukv[:, :, :MLA_NOPE].reshape(MLA_KV_LORA, -1), N_HEADS, MLA_NOPE).astype(BF16),
        "w_uvt": ukv[:, :, MLA_NOPE:].reshape(MLA_KV_LORA, -1).T.astype(BF16),
        "q_norm": q_norm.reshape(1, -1),
        "kv_norm": kv_norm.reshape(1, -1),
        "b_gate": b_gate.reshape(1, N_BRANCH * d),
        "w_b": w_branch,
        "w_o": w_out,
    }


def kernel(x, c, ada_w, ada_b, norm_pre, norm_post, ffn_w_gate, ffn_w_up, ffn_w_down, mix_w_in, mix_b_gate,
           mla_q_norm, mla_w_uq, mla_kv_norm, mla_w_ukv, mix_w_branch, mix_w_out):
    bsz, s, d = x.shape
    assert bsz == 1 and d == D_MODEL and s % ATT_TILE == 0 and s % min(FFN_TILE, s) == 0
    depth = ada_w.shape[0]
    t = SEQ_TILE
    mod = _modulation(c, ada_w, ada_b)
    tabs = _rope_tables(PARTIAL_ROT, s, 0, HEAD_DIM) + _rope_tables(MLA_ROPE, s, MLA_NOPE, HEAD_PAD)
    upper = (jnp.arange(t)[None, :] > jnp.arange(t)[:, None]).astype(BF16)
    xs = x.reshape(s, d)
    wg, wu, wd = ffn_w_gate.astype(BF16), ffn_w_up.astype(BF16), ffn_w_down.astype(BF16)
    w_in, w_branch, w_out = mix_w_in.astype(BF16), mix_w_branch.astype(BF16), mix_w_out.astype(BF16)
    for l in range(depth):
        xs = _ffn(xs, mod[l, 0], norm_pre[l, 0], norm_post[l, 0], wg, wu, wd, l, 0, 0.5)
        wp = _mixer_weights(w_in[l], mix_b_gate[l], mla_q_norm[l], mla_w_uq[l], mla_kv_norm[l], mla_w_ukv[l],
                            w_branch[l], w_out[l])
        qa, ka, vat, sel, qb, kb, vbt, qc, kc, vct = _mixer_proj(xs, mod[l, 1], norm_pre[l, 1], wp, tabs)
        ya = _attention(_moba_kernel, qa, ka, vat, (sel,),
                        [pl.BlockSpec((MOBA_GROUP, s // t, ATT_TILE), lambda g, i: (g, 0, i))], MOBA_GROUP,
                        MOBA_GROUP // PAIR * LANES, "moba_attention")
        yb = _stick_breaking(qb, kb, vbt, upper)
        yc = _attention(_mla_kernel, qc, kc, vct, (), [], MLA_GROUP, MLA_GROUP * HEAD_PAD, "mla_attention")
        xs = _merge(xs, ya, yb, yc, wp, mod[l, 1], norm_pre[l, 1], norm_post[l, 1])
        xs = _ffn(xs, mod[l, 2], norm_pre[l, 2], norm_post[l, 2], wg, wu, wd, l, 1, 0.5)
    return xs.reshape(bsz, s, d)
```
